```python
import jax, jax.numpy as jnp
from jax import lax
import numpy as np

D_MODEL = 2048
BATCH = 2
SEQ = 16384
DEPTH = 2

GROUP = D_MODEL // 4

RET_HEADS = 4
RET_DK = GROUP // (2 * RET_HEADS)
RET_DV = GROUP // RET_HEADS
RET_CHUNK = 128
ROPE_BASE = 10000.0

RWKV_HEAD_DIM = 64
RWKV_HEADS = GROUP // RWKV_HEAD_DIM
RWKV_W_RANK = 32
RWKV_A_RANK = 32
RWKV_G_RANK = 96

GLA_HEADS = 4
GLA_DK = GROUP // (2 * GLA_HEADS)
GLA_DV = GROUP // GLA_HEADS
GLA_GATE_RANK = 16
GLA_TAU = 16.0
GLA_CHUNK = 64

MLSTM_HEADS = 4
MLSTM_DK = GROUP // (2 * MLSTM_HEADS)
MLSTM_DV = GROUP // MLSTM_HEADS
MLSTM_CONV = 4
MLSTM_CHUNK = 64

N_EXPERTS = 32
TOP_K = 4
D_FF = D_MODEL
SWIGLU_ALPHA = 1.702
SWIGLU_LIMIT = 7.0
MOE_BLOCK = 128

DN_ALPHA = (2 * DEPTH) ** 0.25
DN_BETA = (8 * DEPTH) ** -0.25
NORM_EPS = 1e-5

RET_COLS = (RET_HEADS * RET_DK, RET_HEADS * RET_DK, GROUP, GROUP)
RWKV_COLS = (GROUP, GROUP, GROUP, RWKV_W_RANK, RWKV_A_RANK, RWKV_G_RANK)
GLA_COLS = (GLA_HEADS * GLA_DK, GLA_HEADS * GLA_DK, GROUP, GLA_GATE_RANK, GROUP)
MLSTM_COLS = (MLSTM_HEADS * MLSTM_DK, MLSTM_HEADS * MLSTM_DK, GROUP, MLSTM_HEADS, MLSTM_HEADS, GROUP)
RET_WIDTH = sum(RET_COLS)
RWKV_WIDTH = sum(RWKV_COLS)
GLA_WIDTH = sum(GLA_COLS)
MLSTM_WIDTH = sum(MLSTM_COLS)
D_IN = RET_WIDTH + RWKV_WIDTH + GLA_WIDTH + MLSTM_WIDTH

kernel_name = 'hymba_style_retention_rwkv7_gla_mlstm_moe_deepnorm'


def split_cols(p, sizes):
    offs = np.cumsum(sizes)[:-1]
    return jnp.split(p, [int(o) for o in offs], axis=-1)


def layer_norm(x, g, b):
    x32 = x.astype(jnp.float32)
    mu = jnp.mean(x32, -1, keepdims=True)
    var = jnp.mean(jnp.square(x32 - mu), -1, keepdims=True)
    return ((x32 - mu) * lax.rsqrt(var + NORM_EPS) * g + b).astype(x.dtype)


def head_norm(y, g):
    B, T, H, d = y.shape
    y = y.astype(jnp.float32)
    mu = jnp.mean(y, -1, keepdims=True)
    var = jnp.mean(jnp.square(y - mu), -1, keepdims=True)
    return ((y - mu) * lax.rsqrt(var + NORM_EPS)).reshape(B, T, H * d) * g


def to_chunks(x, chunk):
    B, T, H, d = x.shape
    return x.reshape(B, T // chunk, chunk, H, d).transpose(1, 0, 3, 2, 4)


def from_chunks(o):
    NC, B, H, L, d = o.shape
    return o.transpose(1, 0, 3, 2, 4).reshape(B, NC * L, H, d)


def rope(x, pos):
    half = x.shape[-1] // 2
    inv = ROPE_BASE ** (-jnp.arange(half, dtype=jnp.float32) / half)
    ang = pos[:, None] * inv[None, :]
    cos = jnp.cos(ang)[:, None, :]
    sin = jnp.sin(ang)[:, None, :]
    x1, x2 = x[..., :half], x[..., half:]
    return jnp.concatenate([x1 * cos - x2 * sin, x1 * sin + x2 * cos], -1)


def causal_dwconv(x, w):
    K = w.shape[0]
    T = x.shape[1]
    xp = jnp.pad(x, ((0, 0), (K - 1, 0), (0, 0)))
    return sum(xp[:, j:j + T] * w[j] for j in range(K))


def retention(q, k, v, gate, norm_g):
    B, T, _ = q.shape
    H, L = RET_HEADS, RET_CHUNK
    pos = jnp.arange(T, dtype=jnp.float32)
    q = rope(q.reshape(B, T, H, RET_DK), pos) * RET_DK ** -0.5
    k = rope(k.reshape(B, T, H, RET_DK), pos)
    v = v.reshape(B, T, H, RET_DV)
    log_gamma = jnp.log(1.0 - 2.0 ** (-5.0 - jnp.arange(H, dtype=jnp.float32)))
    idx = jnp.arange(L, dtype=jnp.float32)
    rel = idx[:, None] - idx[None, :]
    decay_intra = jnp.where(rel[None] >= 0, jnp.exp(jnp.maximum(rel, 0.0)[None] * log_gamma[:, None, None]), 0.0)
    decay_q = jnp.exp((idx + 1.0)[None, :] * log_gamma[:, None])
    decay_k = jnp.exp((L - 1.0 - idx)[None, :] * log_gamma[:, None])
    decay_chunk = jnp.exp(L * log_gamma)

    def step(state, blk):
        qb, kb, vb = blk
        s = jnp.einsum('bhid,bhjd->bhij', qb, kb) * decay_intra
        o = jnp.einsum('bhij,bhje->bhie', s, vb) + jnp.einsum('bhid,bhde->bhie', qb * decay_q[..., None], state)
        state = decay_chunk[:, None, None] * state + jnp.einsum('bhjd,bhje->bhde', kb * decay_k[..., None], vb)
        return state, o

    s0 = jnp.zeros((B, H, RET_DK, RET_DV), jnp.float32)
    _, o = lax.scan(step, s0, (to_chunks(q, L), to_chunks(k, L), to_chunks(v, L)))
    return head_norm(from_chunks(o), norm_g) * jax.nn.silu(gate)


def rwkv7_time_mix(cols, mu, w0, w2, a0, a2, g2, k_k, k_a, r_k, norm_g):
    prev = jnp.pad(cols, ((0, 0), (1, 0), (0, 0)))[:, :-1]
    cols = cols + (prev - cols) * mu
    r, k, v, wd, ad, gd = split_cols(cols, RWKV_COLS)
    B, T, _ = r.shape
    H, d = RWKV_HEADS, RWKV_HEAD_DIM
    w = -jax.nn.softplus(-(w0 + jnp.tanh(wd) @ w2)) - 0.5
    decay = jnp.exp(-jnp.exp(w))
    a = jax.nn.sigmoid(a0 + ad @ a2)
    g = jax.nn.sigmoid(gd) @ g2
    heads = lambda t: t.reshape(B, T, H, d)
    kk = heads(k * k_k)
    kk = kk / jnp.maximum(jnp.sqrt(jnp.sum(kk * kk, -1, keepdims=True)), 1e-12)
    k = k * (1.0 + (a - 1.0) * k_a)
    r, k, v, decay, a = heads(r), heads(k), heads(v), heads(decay), heads(a)

    def step(S, inp):
        r_t, w_t, k_t, v_t, kk_t, a_t = inp
        sa = jnp.einsum('bhvk,bhk->bhv', S, -kk_t)
        S = S * w_t[:, :, None, :] + sa[..., None] * (kk_t * a_t)[:, :, None, :] + v_t[..., None] * k_t[:, :, None, :]
        return S, jnp.einsum('bhvk,bhk->bhv', S, r_t)

    tm = lambda t: jnp.moveaxis(t, 1, 0)
    S0 = jnp.zeros((B, H, d, d), jnp.float32)
    _, y = lax.scan(step, S0, (tm(r), tm(decay), tm(k), tm(v), tm(kk), tm(a)))
    y = jnp.moveaxis(y, 0, 1)
    bonus = jnp.sum(r * k * r_k.reshape(H, d), -1, keepdims=True) * v
    return (head_norm(y, norm_g) + bonus.reshape(B, T, H * d)) * g


def gla(q, k, v, gd, gs, g2, gb, norm_g):
    B, T, _ = q.shape
    H, L = GLA_HEADS, GLA_CHUNK
    log_alpha = jax.nn.log_sigmoid(gd @ g2 + gb) / GLA_TAU
    q = q.reshape(B, T, H, GLA_DK) * GLA_DK ** -0.5
    k = k.reshape(B, T, H, GLA_DK)
    la = log_alpha.reshape(B, T, H, GLA_DK)
    v = v.reshape(B, T, H, GLA_DV)
    causal = jnp.tril(jnp.ones((L, L), bool))[:, :, None]

    def step(S, blk):
        qb, kb, vb, lab = blk
        b = jnp.cumsum(lab, axis=2)
        diff = b[:, :, :, None, :] - b[:, :, None, :, :]
        wt = jnp.exp(jnp.where(causal, diff, -jnp.inf))
        A = jnp.einsum('bhijd,bhjd->bhij', qb[:, :, :, None, :] * wt, kb)
        o = jnp.einsum('bhij,bhje->bhie', A, vb) + jnp.einsum('bhid,bhde->bhie', qb * jnp.exp(b), S)
        b_last = b[:, :, -1:, :]
        S = jnp.exp(b_last[:, :, 0, :])[..., None] * S + jnp.einsum('bhjd,bhje->bhde', kb * jnp.exp(b_last - b), vb)
        return S, o

    S0 = jnp.zeros((B, H, GLA_DK, GLA_DV), jnp.float32)
    _, o = lax.scan(step, S0, (to_chunks(q, L), to_chunks(k, L), to_chunks(v, L), to_chunks(la, L)))
    return head_norm(from_chunks(o), norm_g) * jax.nn.silu(gs)


def mlstm(q, k, v, ig, fg, og, conv_w, ib, fb, norm_g):
    B, T, _ = q.shape
    H, L = MLSTM_HEADS, MLSTM_CHUNK
    qk = jax.nn.silu(causal_dwconv(jnp.concatenate([q, k], -1), conv_w))
    q, k = jnp.split(qk, 2, axis=-1)
    q = q.reshape(B, T, H, MLSTM_DK)
    k = k.reshape(B, T, H, MLSTM_DK) * MLSTM_DK ** -0.5
    v = v.reshape(B, T, H, MLSTM_DV)
    i_pre = ig + ib
    log_f = jax.nn.log_sigmoid(fg + fb)
    causal = jnp.tril(jnp.ones((L, L), bool))

    def step(carry, blk):
        C, n, m = carry
        qb, kb, vb, ib_, lfb = blk
        b = jnp.cumsum(lfb, axis=-1)
        log_d = jnp.where(causal, b[..., :, None] - b[..., None, :] + ib_[..., None, :], -jnp.inf)
        m_inter = b + m[..., None]
        m_row = jnp.maximum(jnp.max(log_d, -1), m_inter)
        dmat = jnp.exp(log_d - m_row[..., None])
        inter = jnp.exp(m_inter - m_row)
        s = jnp.einsum('bhid,bhjd->bhij', qb, kb) * dmat
        num = jnp.einsum('bhij,bhje->bhie', s, vb) + inter[..., None] * jnp.einsum('bhid,bhde->bhie', qb, C)
        den = jnp.sum(s, -1) + inter * jnp.einsum('bhid,bhd->bhi', qb, n)
        h = num / jnp.maximum(jnp.abs(den), jnp.exp(-m_row))[..., None]
        m_new = m_row[..., -1]
        wj = jnp.exp(b[..., -1:] - b + ib_ - m_new[..., None])
        carry_scale = jnp.exp(b[..., -1] + m - m_new)
        C = carry_scale[..., None, None] * C + jnp.einsum('bhjd,bhje->bhde', kb * wj[..., None], vb)
        n = carry_scale[..., None] * n + jnp.einsum('bhj,bhjd->bhd', wj, kb)
        return (C, n, m_new), h

    carry0 = (jnp.zeros((B, H, MLSTM_DK, MLSTM_DV), jnp.float32),
              jnp.zeros((B, H, MLSTM_DK), jnp.float32),
              jnp.zeros((B, H), jnp.float32))
    xs = (to_chunks(q, L), to_chunks(k, L), to_chunks(v, L),
          to_chunks(i_pre[..., None], L)[..., 0], to_chunks(log_f[..., None], L)[..., 0])
    _, h = lax.scan(step, carry0, xs)
    return head_norm(from_chunks(h), norm_g) * jax.nn.sigmoid(og)


def clamped_swiglu(h):
    glu, lin = jnp.split(h, 2, axis=-1)
    glu = jnp.minimum(glu, SWIGLU_LIMIT)
    lin = jnp.clip(lin, -SWIGLU_LIMIT, SWIGLU_LIMIT)
    return glu * jax.nn.sigmoid(SWIGLU_ALPHA * glu) * (lin + 1.0)


def moe(x, router_w, router_b, w_gu, b_gu, w_dn, b_dn):
    B, T, D = x.shape
    xf = x.reshape(B * T, D)
    n_tok = B * T
    logits = (xf @ router_w).astype(jnp.float32) + router_b
    top_logit, top_idx = lax.top_k(logits, TOP_K)
    gate = jax.nn.softmax(top_logit, axis=-1)
    n_assign = n_tok * TOP_K
    flat_e = top_idx.reshape(-1)
    flat_tok = jnp.arange(n_assign, dtype=jnp.int32) // TOP_K
    flat_gate = gate.reshape(-1)
    order = jnp.argsort(flat_e)
    se = flat_e[order]
    counts = jnp.bincount(flat_e, length=N_EXPERTS)
    starts = jnp.cumsum(counts) - counts
    pcounts = (counts + MOE_BLOCK - 1) // MOE_BLOCK * MOE_BLOCK
    pends = jnp.cumsum(pcounts)
    pstarts = pends - pcounts
    dest = pstarts[se] + jnp.arange(n_assign, dtype=jnp.int32) - starts[se]
    n_slots = -(-n_assign // MOE_BLOCK) * MOE_BLOCK + N_EXPERTS * MOE_BLOCK
    n_blocks = n_slots // MOE_BLOCK
    slot_tok = jnp.zeros((n_slots,), jnp.int32).at[dest].set(flat_tok[order])
    slot_gate = jnp.zeros((n_slots,), jnp.float32).at[dest].set(flat_gate[order])
    block_start = jnp.arange(n_blocks, dtype=jnp.int32) * MOE_BLOCK
    block_exp = jnp.minimum(jnp.searchsorted(pends, block_start, side='right'), N_EXPERTS - 1).astype(jnp.int32)

    def body(y, blk):
        tok, g, e = blk
        h = xf[tok] @ w_gu[e] + b_gu[e]
        out = clamped_swiglu(h) @ w_dn[e] + b_dn[e]
        return y.at[tok].add(out * g.astype(out.dtype)[:, None]), None

    y0 = jnp.zeros_like(xf)
    y, _ = lax.scan(body, y0, (slot_tok.reshape(n_blocks, MOE_BLOCK),
                               slot_gate.reshape(n_blocks, MOE_BLOCK), block_exp))
    return y.reshape(B, T, D).astype(x.dtype)


def setup_inputs(seed: int = 0) -> dict:
    key = jax.random.key(seed)
    ks = iter(jax.random.split(key, 40))
    nrm = lambda shape, scale: jax.random.normal(next(ks), shape, jnp.float32) * scale
    uni = lambda shape, lo, hi: jax.random.uniform(next(ks), shape, jnp.float32, lo, hi)
    L = DEPTH
    D_MIX = 4 * GROUP
    return {
        'x': nrm((BATCH, SEQ, D_MODEL), 1.0),
        'w_in': nrm((L, D_MODEL, D_IN), D_MODEL ** -0.5),
        'rwkv_mu': uni((L, RWKV_WIDTH), 0.0, 1.0),
        'rwkv_w0': uni((L, GROUP), -6.5, -1.5),
        'rwkv_w2': nrm((L, RWKV_W_RANK, GROUP), 0.1 * RWKV_W_RANK ** -0.5),
        'rwkv_a0': nrm((L, GROUP), 0.1),
        'rwkv_a2': nrm((L, RWKV_A_RANK, GROUP), 0.1 * RWKV_A_RANK ** -0.5),
        'rwkv_g2': nrm((L, RWKV_G_RANK, GROUP), RWKV_G_RANK ** -0.5),
        'rwkv_k_k': uni((L, GROUP), 0.7, 1.0),
        'rwkv_k_a': uni((L, GROUP), 0.8, 1.2),
        'rwkv_r_k': nrm((L, GROUP), 0.1),
        'gla_g2': nrm((L, GLA_GATE_RANK, GLA_HEADS * GLA_DK), GLA_GATE_RANK ** -0.5),
        'gla_gb': uni((L, GLA_HEADS * GLA_DK), 0.0, 4.0),
        'mlstm_conv': nrm((L, MLSTM_CONV, 2 * MLSTM_HEADS * MLSTM_DK), 0.5),
        'mlstm_ib': nrm((L, MLSTM_HEADS), 0.1),
        'mlstm_fb': uni((L, MLSTM_HEADS), 3.0, 6.0),
        'head_norm_g': 1.0 + nrm((L, D_MIX), 0.02),
        'w_out': nrm((L, D_MIX, D_MODEL), DN_BETA * D_MIX ** -0.5),
        'ln1_g': 1.0 + nrm((L, D_MODEL), 0.02),
        'ln1_b': nrm((L, D_MODEL), 0.02),
        'router_w': nrm((L, D_MODEL, N_EXPERTS), D_MODEL ** -0.5),
        'router_b': nrm((L, N_EXPERTS), 0.01),
        'w_gu': nrm((L, N_EXPERTS, D_MODEL, 2 * D_FF), D_MODEL ** -0.5),
        'b_gu': nrm((L, N_EXPERTS, 2 * D_FF), 0.01),
        'w_dn': nrm((L, N_EXPERTS, D_FF, D_MODEL), DN_BETA * D_FF ** -0.5),
        'b_dn': nrm((L, N_EXPERTS, D_MODEL), 0.01),
        'ln2_g': 1.0 + nrm((L, D_MODEL), 0.02),
        'ln2_b': nrm((L, D_MODEL), 0.02),
    }


def reference(x, w_in, rwkv_mu, rwkv_w0, rwkv_w2, rwkv_a0, rwkv_a2, rwkv_g2, rwkv_k_k, rwkv_k_a, rwkv_r_k,
              gla_g2, gla_gb, mlstm_conv, mlstm_ib, mlstm_fb, head_norm_g, w_out, ln1_g, ln1_b,
              router_w, router_b, w_gu, b_gu, w_dn, b_dn, ln2_g, ln2_b):
    for l in range(DEPTH):
        p = (x @ w_in[l]).astype(jnp.float32)
        p_ret, p_rwkv, p_gla, p_mlstm = split_cols(p, (RET_WIDTH, RWKV_WIDTH, GLA_WIDTH, MLSTM_WIDTH))
        g_ret, g_rwkv, g_gla, g_mlstm = jnp.split(head_norm_g[l], 4)
        y_ret = retention(*split_cols(p_ret, RET_COLS), g_ret)
        y_rwkv = rwkv7_time_mix(p_rwkv, rwkv_mu[l], rwkv_w0[l], rwkv_w2[l], rwkv_a0[l], rwkv_a2[l],
                                rwkv_g2[l], rwkv_k_k[l], rwkv_k_a[l], rwkv_r_k[l], g_rwkv)
        y_gla = gla(*split_cols(p_gla, GLA_COLS), gla_g2[l], gla_gb[l], g_gla)
        y_mlstm = mlstm(*split_cols(p_mlstm, MLSTM_COLS), mlstm_conv[l], mlstm_ib[l], mlstm_fb[l], g_mlstm)
        y_mix = jnp.concatenate([y_ret, y_rwkv, y_gla, y_mlstm], -1).astype(x.dtype) @ w_out[l]
        x = layer_norm(DN_ALPHA * x + y_mix, ln1_g[l], ln1_b[l])
        y_moe = moe(x, router_w[l], router_b[l], w_gu[l], b_gu[l], w_dn[l], b_dn[l])
        x = layer_norm(DN_ALPHA * x + y_moe, ln2_g[l], ln2_b[l])
    return x
```

```python
import functools
import math

import numpy as np
import jax
import jax.numpy as jnp
from jax import lax
from jax.experimental import pallas as pl
from jax.experimental.pallas import tpu as pltpu

F32 = jnp.float32
BF16 = jnp.bfloat16
HI = lax.Precision.HIGHEST

D_MODEL = 2048
GROUP = 512
N_EXPERTS = 32
TOP_K = 4
D_FF = 2048
DEPTH = 2
NORM_EPS = 1e-5
DN_ALPHA = (2 * DEPTH) ** 0.25
ROPE_BASE = 10000.0
GLA_TAU = 16.0
SWIGLU_ALPHA = 1.702
SWIGLU_LIMIT = 7.0

LANE = 128
TOKEN_BLOCK = 256
VMEM_LIMIT = 56 * 1024 * 1024

RET_OFF, RWKV_OFF, GLA_OFF, MLSTM_OFF, D_IN = 0, 1536, 3232, 4784, 6328
RET_W, RWKV_W, GLA_W, MLSTM_W = 1536, 1792, 1664, 1664


def _ret_cols():
    def half_split(base):
        lo = [base + h * 64 + j for h in range(4) for j in range(32)]
        hi = [base + h * 64 + 32 + j for h in range(4) for j in range(32)]
        return lo + hi
    cols = half_split(RET_OFF) + half_split(RET_OFF + 256)
    cols += list(range(RET_OFF + 512, RET_OFF + 1536))
    return np.asarray(cols, np.int32)


def _rwkv_cols():
    cols = list(range(RWKV_OFF, RWKV_OFF + 1696)) + [D_IN] * 96
    return np.asarray(cols, np.int32)


def _gla_cols():
    o = GLA_OFF
    cols = list(range(o, o + 1024)) + list(range(o + 1040, o + 1552)) + list(range(o + 1024, o + 1040)) + [D_IN] * 112
    return np.asarray(cols, np.int32)


def _mlstm_cols():
    o = MLSTM_OFF
    cols = list(range(o, o + 1024)) + list(range(o + 1032, o + 1544)) + list(range(o + 1024, o + 1032)) + [D_IN] * 120
    return np.asarray(cols, np.int32)


def _dot(a, b):
    return jnp.dot(a.astype(BF16), b.astype(BF16), preferred_element_type=F32)


def _dot_nt(a, b):
    return lax.dot_general(a.astype(BF16), b.astype(BF16), (((1,), (1,)), ((), ())), preferred_element_type=F32)


def _dot_tn(a, b):
    return lax.dot_general(a.astype(BF16), b.astype(BF16), (((0,), (0,)), ((), ())), preferred_element_type=F32)


def _dot_hi(a, b):
    return jnp.dot(a, b, precision=HI, preferred_element_type=F32)


def _silu(x):
    return x * jax.nn.sigmoid(x)


def _head_norm_lanes(o, width):
    parts = []
    for h in range(o.shape[1] // width):
        oh = o[:, h * width:(h + 1) * width]
        mu = jnp.mean(oh, axis=-1, keepdims=True)
        d = oh - mu
        var = jnp.mean(d * d, axis=-1, keepdims=True)
        parts.append(d * lax.rsqrt(var + NORM_EPS))
    return jnp.concatenate(parts, axis=1)


RET_L = 128


def _ret_tables():
    lg = np.log(1.0 - 2.0 ** (-5.0 - np.arange(4, dtype=np.float64)))
    i = np.arange(RET_L, dtype=np.float64)
    lane = np.arange(256)
    head_of_lane = (lane % 128) // 32
    ms = np.zeros((4 * RET_L, 256), np.float32)
    mq = np.zeros((4 * RET_L, 256), np.float32)
    dk = np.zeros((4 * RET_L, 256), np.float32)
    di = np.zeros((4 * RET_L, RET_L), np.float32)
    for h in range(4):
        m = (head_of_lane == h).astype(np.float64)
        ms[h * RET_L:(h + 1) * RET_L] = m[None, :]
        mq[h * RET_L:(h + 1) * RET_L] = np.exp((i + 1.0) * lg[h])[:, None] * m[None, :]
        dk[h * RET_L:(h + 1) * RET_L] = np.exp((RET_L - 1.0 - i) * lg[h])[:, None]
        rel = i[:, None] - i[None, :]
        di[h * RET_L:(h + 1) * RET_L] = np.where(rel >= 0, np.exp(np.maximum(rel, 0.0) * lg[h]), 0.0)
    dchunk = np.exp(RET_L * lg).astype(np.float32)
    return ms, mq, dk, di, dchunk


def _ret_kernel(x_ref, w_ref, cos_ref, sin_ref, ms_ref, mq_ref, dk_ref, di_ref, g_ref, y_ref, s_ref, *, dchunk):
    @pl.when(pl.program_id(1) == 0)
    def _():
        s_ref[...] = jnp.zeros_like(s_ref)

    p = jnp.dot(x_ref[0], w_ref[...], preferred_element_type=F32)
    L = RET_L
    for c in range(TOKEN_BLOCK // L):
        sl = slice(c * L, (c + 1) * L)
        q, k = p[sl, 0:256], p[sl, 256:512]
        v, gate = p[sl, 512:1024], p[sl, 1024:1536]
        cs, sn = cos_ref[sl, :], sin_ref[sl, :]

        def rope(t):
            t1, t2 = t[:, :128], t[:, 128:]
            return jnp.concatenate([t1 * cs - t2 * sn, t1 * sn + t2 * cs], axis=1)

        qr = rope(q) * (64.0 ** -0.5)
        kr = rope(k)
        q4 = jnp.concatenate([qr] * 4, axis=0)
        scores = _dot_nt(q4 * ms_ref[...], kr) * di_ref[...]
        qd = q4 * mq_ref[...]
        k4 = jnp.concatenate([kr] * 4, axis=0) * dk_ref[...]
        outs = []
        for h in range(4):
            hb = slice(h * L, (h + 1) * L)
            vb = slice(h * 128, (h + 1) * 128)
            vh = v[:, vb]
            sh = s_ref[:, vb]
            o = _dot(scores[hb], vh) + _dot(qd[hb], sh)
            outs.append(o)
            s_ref[:, vb] = dchunk[h] * sh + _dot_tn(k4[hb], vh)
        o = _head_norm_lanes(jnp.concatenate(outs, axis=1), 128)
        y_ref[0, sl, :] = (o * g_ref[...] * _silu(gate)).astype(y_ref.dtype)


def _retention(xb, w, cos, sin, g, tables):
    B, T, _ = xb.shape
    ms, mq, dk, di, dchunk = tables
    const = lambda shape: pl.BlockSpec(shape, lambda b, t: (0,) * len(shape))
    return pl.pallas_call(
        functools.partial(_ret_kernel, dchunk=tuple(float(d) for d in dchunk)),
        grid=(B, T // TOKEN_BLOCK),
        in_specs=[
            pl.BlockSpec((1, TOKEN_BLOCK, D_MODEL), lambda b, t: (b, t, 0)),
            const((D_MODEL, RET_W)),
            pl.BlockSpec((TOKEN_BLOCK, 128), lambda b, t: (t, 0)),
            pl.BlockSpec((TOKEN_BLOCK, 128), lambda b, t: (t, 0)),
            const((4 * RET_L, 256)), const((4 * RET_L, 256)), const((4 * RET_L, 256)), const((4 * RET_L, RET_L)),
            const((1, GROUP)),
        ],
        out_specs=pl.BlockSpec((1, TOKEN_BLOCK, GROUP), lambda b, t: (b, t, 0)),
        out_shape=jax.ShapeDtypeStruct((B, T, GROUP), BF16),
        scratch_shapes=[pltpu.VMEM((256, GROUP), F32)],
        compiler_params=pltpu.CompilerParams(dimension_semantics=("arbitrary", "arbitrary"),
                                             vmem_limit_bytes=VMEM_LIMIT),
        name="retention",
    )(xb, w, cos, sin, jnp.asarray(ms), jnp.asarray(mq), jnp.asarray(dk), jnp.asarray(di), g)


RWKV_L = 64


def _rwkv_tables():
    lane = np.arange(256)
    row = np.arange(256)
    ms4 = (lane[None, :] // 64 == row[:, None] // 64).astype(np.float32)
    same = (row[:, None] // 64 == row[None, :] // 64)
    tt, ii = row[:, None] % 64, row[None, :] % 64
    strict = (same & (tt > ii)).astype(np.float32)
    incl = (same & (tt >= ii)).astype(np.float32)
    tri = incl.copy()
    r5 = np.arange(512)
    bd = (r5[:, None] // 64 == r5[None, :] // 64).astype(np.float32)
    return ms4, strict, incl, tri, bd


def _rwkv_kernel(x_ref, w_ref, mu_ref, w0_ref, w2_ref, a0_ref, a2_ref, g2_ref, kk_ref, ka_ref, rk_ref, gn_ref,
                 ms4_ref, strict_ref, incl_ref, tri_ref, bd_ref, y_ref, p_scr, s_scr, o_scr):
    TC, L = TOKEN_BLOCK, RWKV_L

    @pl.when(pl.program_id(1) == 0)
    def _():
        s_scr[...] = jnp.zeros_like(s_scr)
        p_scr[pl.ds(0, 8), :] = jnp.zeros((8, RWKV_W), F32)

    p = jnp.dot(x_ref[0], w_ref[...], preferred_element_type=F32)
    p_scr[pl.ds(8, TC), :] = p
    prev = p_scr[pl.ds(7, TC), :]
    p_scr[pl.ds(0, 8), :] = p[TC - 8:, :]
    cols = p + (prev - p) * mu_ref[...]
    r, k, v = cols[:, 0:512], cols[:, 512:1024], cols[:, 1024:1536]
    tail = cols[:, 1536:1792]
    w = -jax.nn.softplus(-(w0_ref[...] + _dot_hi(jnp.tanh(tail), w2_ref[...]))) - 0.5
    logdec = -jnp.exp(w)
    a = jax.nn.sigmoid(a0_ref[...] + _dot_hi(tail, a2_ref[...]))
    g = _dot_hi(jax.nn.sigmoid(tail), g2_ref[...])
    kk = k * kk_ref[...]
    bd = bd_ref[...]
    kk = kk / jnp.maximum(jnp.sqrt(_dot_hi(kk * kk, bd)), 1e-12)
    k = k * (1.0 + (a - 1.0) * ka_ref[...])
    bonus = _dot_hi(r * k * rk_ref[...], bd) * v
    cum = _dot_hi(tri_ref[...], logdec)
    beta = kk * a
    ms4 = ms4_ref[...]
    strict, incl = strict_ref[...], incl_ref[...]

    def stack(t):
        return jnp.concatenate([t] * 4, axis=0) * ms4

    for c in range(TC // L):
        rows = slice(c * L, (c + 1) * L)
        for gi in range(2):
            lanes = slice(gi * 256, (gi + 1) * 256)
            cm, lw = cum[rows, lanes], logdec[rows, lanes]
            cl = cm[L - 1:L, :]
            gam, gam_ex, igam, gam_r = jnp.exp(cm), jnp.exp(cm - lw), jnp.exp(-cm), jnp.exp(cl - cm)
            kk_c, r_c, k_c, v_c, b_c = kk[rows, lanes], r[rows, lanes], k[rows, lanes], v[rows, lanes], beta[rows, lanes]
            x1, x2 = stack(-kk_c * gam_ex), stack(r_c * gam)
            y1, y2 = stack(b_c * igam), stack(k_c * igam)
            z1, z2 = stack(b_c * gam_r), stack(k_c * gam_r)
            sv = stack(v_c)
            s0 = s_scr[gi]
            ab = _dot_nt(jnp.concatenate([x1, x2], axis=0), jnp.concatenate([y1, y2], axis=0))
            a_b, a_k = ab[0:256, 0:256] * strict, ab[0:256, 256:512] * strict
            b_b, b_k = ab[256:512, 0:256] * incl, ab[256:512, 256:512] * incl
            u = _dot(a_k, sv) + _dot_nt(x1, s0)
            pw = a_b
            for it in range(6):
                u = u + _dot(pw, u)
                if it < 5:
                    pw = _dot(pw, pw)
            yst = _dot_nt(x2, s0) + _dot(b_b, u) + _dot(b_k, sv)
            o_scr[rows, lanes] = yst[0:64] + yst[64:128] + yst[128:192] + yst[192:256]
            s_scr[gi] = s0 * jnp.exp(cl) + _dot_tn(u, z1) + _dot_tn(sv, z2)

    y = o_scr[...]
    mean = _dot_hi(y, bd) * (1.0 / 64.0)
    d = y - mean
    var = _dot_hi(d * d, bd) * (1.0 / 64.0)
    yn = d * lax.rsqrt(var + NORM_EPS) * gn_ref[...]
    y_ref[0] = ((yn + bonus) * g).astype(y_ref.dtype)


def _rwkv(xb, w, mu, w0, w2p, a0, a2p, g2p, k_k, k_a, r_k, gn, tables):
    B, T, _ = xb.shape
    const = lambda shape: pl.BlockSpec(shape, lambda b, t: (0,) * len(shape))
    row = lambda n: const((1, n))
    return pl.pallas_call(
        _rwkv_kernel,
        grid=(B, T // TOKEN_BLOCK),
        in_specs=[
            pl.BlockSpec((1, TOKEN_BLOCK, D_MODEL), lambda b, t: (b, t, 0)),
            const((D_MODEL, RWKV_W)), row(RWKV_W), row(GROUP), const((256, GROUP)), row(GROUP), const((256, GROUP)),
            const((256, GROUP)), row(GROUP), row(GROUP), row(GROUP), row(GROUP),
            const((256, 256)), const((256, 256)), const((256, 256)), const((256, 256)), const((512, 512)),
        ],
        out_specs=pl.BlockSpec((1, TOKEN_BLOCK, GROUP), lambda b, t: (b, t, 0)),
        out_shape=jax.ShapeDtypeStruct((B, T, GROUP), BF16),
        scratch_shapes=[pltpu.VMEM((TOKEN_BLOCK + 8, RWKV_W), F32), pltpu.VMEM((2, 256, 256), F32),
                        pltpu.VMEM((TOKEN_BLOCK, GROUP), F32)],
        compiler_params=pltpu.CompilerParams(dimension_semantics=("arbitrary", "arbitrary"),
                                             vmem_limit_bytes=VMEM_LIMIT),
        name="rwkv7",
    )(xb, w, mu, w0, w2p, a0, a2p, g2p, k_k, k_a, r_k, gn, *[jnp.asarray(t) for t in tables])


GLA_SUB = 16


def _gla_tables():
    r = np.arange(256)
    tri16 = ((r[:, None] // 16 == r[None, :] // 16) & (r[:, None] >= r[None, :])).astype(np.float32)
    sel = (r[:, None] // 64 == np.arange(512)[None, :] // 128).astype(np.float32)
    ms = (np.arange(256)[None, :] // 64 == np.arange(64)[:, None] // 16).astype(np.float32)
    return tri16, sel, ms


def _gla_kernel(x_ref, w_ref, g2_ref, gb_ref, gn_ref, tri_ref, sel_ref, ms_ref, y_ref,
                q_scr, k_scr, v_scr, b_scr, o_scr, st_scr):
    TC, SUB = TOKEN_BLOCK, GLA_SUB

    @pl.when(pl.program_id(1) == 0)
    def _():
        st_scr[...] = jnp.zeros_like(st_scr)

    p = jnp.dot(x_ref[0], w_ref[...], preferred_element_type=F32)
    gate = p[:, 1024:1536]
    la = jax.nn.log_sigmoid(_dot_hi(p[:, 1536:1664], g2_ref[...]) + gb_ref[...]) * (1.0 / GLA_TAU)
    q_scr[...] = p[:, 0:256] * (64.0 ** -0.5)
    k_scr[...] = p[:, 256:512]
    v_scr[...] = p[:, 512:1024]
    b_scr[...] = _dot_hi(tri_ref[...], la)
    sel = sel_ref[...].astype(BF16)
    ms = ms_ref[...]
    row = lax.broadcasted_iota(jnp.int32, (SUB, 1), 0)

    def body(s, carry):
        rows = pl.ds(pl.multiple_of(s * SUB, SUB), SUB)
        qb, kb, vb, bb = q_scr[rows, :], k_scr[rows, :], v_scr[rows, :], b_scr[rows, :]
        bl = bb[SUB - 1:SUB, :]
        qe = jnp.concatenate([qb * jnp.exp(bb)] * 4, axis=0) * ms
        o_inter = jnp.concatenate(
            [_dot_nt(qe[h * SUB:(h + 1) * SUB], st_scr[h * 128:(h + 1) * 128, :]) for h in range(4)], axis=1)
        ts = []
        for j in range(SUB):
            diff = jnp.where(row >= j, bb - bb[j:j + 1, :], -1e30)
            ts.append(qb * kb[j:j + 1, :] * jnp.exp(diff))
        rr = jnp.dot(jnp.concatenate(ts, axis=0).astype(BF16), sel, preferred_element_type=F32)
        o_intra = rr[0:SUB] * vb[0:1, :]
        for j in range(1, SUB):
            o_intra = o_intra + rr[j * SUB:(j + 1) * SUB] * vb[j:j + 1, :]
        o_scr[rows, :] = o_intra + o_inter
        st_scr[...] = st_scr[...] * jnp.exp(bl) + _dot_tn(vb, kb * jnp.exp(bl - bb))
        return carry

    lax.fori_loop(0, TC // SUB, body, 0)
    o = _head_norm_lanes(o_scr[...], 128)
    y_ref[0] = (o * gn_ref[...] * _silu(gate)).astype(y_ref.dtype)


def _gla(xb, w, g2p, gb, gn, tables):
    B, T, _ = xb.shape
    const = lambda shape: pl.BlockSpec(shape, lambda b, t: (0,) * len(shape))
    return pl.pallas_call(
        _gla_kernel,
        grid=(B, T // TOKEN_BLOCK),
        in_specs=[
            pl.BlockSpec((1, TOKEN_BLOCK, D_MODEL), lambda b, t: (b, t, 0)),
            const((D_MODEL, GLA_W)), const((128, 256)), const((1, 256)), const((1, GROUP)),
            const((256, 256)), const((256, 512)), const((64, 256)),
        ],
        out_specs=pl.BlockSpec((1, TOKEN_BLOCK, GROUP), lambda b, t: (b, t, 0)),
        out_shape=jax.ShapeDtypeStruct((B, T, GROUP), BF16),
        scratch_shapes=[pltpu.VMEM((TOKEN_BLOCK, 256), F32), pltpu.VMEM((TOKEN_BLOCK, 256), F32),
                        pltpu.VMEM((TOKEN_BLOCK, GROUP), F32), pltpu.VMEM((TOKEN_BLOCK, 256), F32),
                        pltpu.VMEM((TOKEN_BLOCK, GROUP), F32), pltpu.VMEM((GROUP, 256), F32)],
        compiler_params=pltpu.CompilerParams(dimension_semantics=("arbitrary", "arbitrary"),
                                             vmem_limit_bytes=VMEM_LIMIT),
        name="gla",
    )(xb, w, g2p, gb, gn, *[jnp.asarray(t) for t in tables])


MLSTM_L = 128


def _mlstm_tables():
    r = np.arange(TOKEN_BLOCK)
    tri = ((r[:, None] // MLSTM_L == r[None, :] // MLSTM_L) & (r[:, None] >= r[None, :])).astype(np.float32)
    ms = (np.arange(256)[None, :] // 64 == np.arange(4 * MLSTM_L)[:, None] // MLSTM_L).astype(np.float32)
    return tri, ms


def _mlstm_kernel(x_ref, w_ref, conv_ref, gbias_ref, gn_ref, tri_ref, ms_ref, y_ref,
                  qk_scr, c_scr, n_scr, m_scr):
    TC, L = TOKEN_BLOCK, MLSTM_L

    @pl.when(pl.program_id(1) == 0)
    def _():
        c_scr[...] = jnp.zeros_like(c_scr)
        n_scr[...] = jnp.zeros_like(n_scr)
        m_scr[...] = jnp.zeros_like(m_scr)
        qk_scr[pl.ds(0, 8), :] = jnp.zeros((8, 512), F32)

    p = jnp.dot(x_ref[0], w_ref[...], preferred_element_type=F32)
    qk_scr[pl.ds(8, TC), :] = p[:, 0:512]
    conv = p[:, 0:512] * conv_ref[3:4, :]
    for j in range(3):
        conv = conv + qk_scr[pl.ds(5 + j, TC), :] * conv_ref[j:j + 1, :]
    qk_scr[pl.ds(0, 8), :] = p[TC - 8:, 0:512]
    qk = _silu(conv)
    q, k = qk[:, 0:256], qk[:, 256:512] * (64.0 ** -0.5)
    v, og = p[:, 512:1024], p[:, 1024:1536]
    tail = p[:, 1536:1664] + gbias_ref[...]
    lane = lax.broadcasted_iota(jnp.int32, (1, 128), 1)
    logf = jnp.where((lane >= 4) & (lane < 8), jax.nn.log_sigmoid(tail), 0.0)
    cumf = _dot_hi(tri_ref[...], logf)
    gcol = jnp.where(lane < 4, tail, cumf)
    eye = (lax.broadcasted_iota(jnp.int32, (128, 128), 0) == lax.broadcasted_iota(jnp.int32, (128, 128), 1)).astype(F32)
    ms = ms_ref[...]
    ri = lax.broadcasted_iota(jnp.int32, (L, L), 0)
    ci = lax.broadcasted_iota(jnp.int32, (L, L), 1)
    causal = ri >= ci
    lane256 = lax.broadcasted_iota(jnp.int32, (1, 256), 1)

    for c in range(TC // L):
        rows = slice(c * L, (c + 1) * L)
        gc = gcol[rows, :]
        gr = lax.dot_general(eye, gc, (((1,), (1,)), ((), ())), precision=HI, preferred_element_type=F32)
        qc, kc, vc = q[rows], k[rows], v[rows]
        q4 = jnp.concatenate([qc] * 4, axis=0) * ms
        scores = _dot_nt(q4, kc)
        qn = jnp.sum(q4 * n_scr[...], axis=1, keepdims=True)
        outs, wrows, n_scale = [], [], jnp.zeros((1, 256), F32)
        for h in range(4):
            hb = slice(h * L, (h + 1) * L)
            vb = slice(h * 128, (h + 1) * 128)
            b_col, i_col = gc[:, 4 + h:5 + h], gc[:, h:h + 1]
            b_row, i_row = gr[4 + h:5 + h, :], gr[h:h + 1, :]
            m_prev = m_scr[0:1, h:h + 1]
            log_d = jnp.where(causal, b_col - b_row + i_row, -jnp.inf)
            m_inter = b_col + m_prev
            m_row = jnp.maximum(jnp.max(log_d, axis=-1, keepdims=True), m_inter)
            dmat = jnp.exp(log_d - m_row)
            inter = jnp.exp(m_inter - m_row)
            s = scores[hb] * dmat
            vh = vc[:, vb]
            ch = c_scr[:, vb]
            num = _dot(s, vh) + inter * _dot(q4[hb], ch)
            den = jnp.sum(s, axis=-1, keepdims=True) + inter * qn[hb]
            outs.append(num / jnp.maximum(jnp.abs(den), jnp.exp(-m_row)))
            m_new = m_row[L - 1:L, :]
            b_last = b_col[L - 1:L, :]
            wj_col = jnp.exp(b_last - b_col + i_col - m_new)
            wj_row = jnp.exp(b_last - b_row + i_row - m_new)
            scale = jnp.exp(b_last + m_prev - m_new)
            c_scr[:, vb] = scale * ch + _dot_tn(kc * wj_col, vh)
            head_lanes = lane256 // 64 == h
            wrows.append(wj_row)
            n_scale = jnp.where(head_lanes, scale, n_scale)
            m_scr[0:1, h:h + 1] = m_new
        wk = _dot_hi(jnp.concatenate(wrows + [jnp.zeros((4, L), F32)], axis=0), kc)
        n_add = jnp.zeros((1, 256), F32)
        for h in range(4):
            n_add = jnp.where(lane256 // 64 == h, wk[h:h + 1, :], n_add)
        n_scr[...] = n_scale * n_scr[...] + n_add
        o = _head_norm_lanes(jnp.concatenate(outs, axis=1), 128)
        y_ref[0, rows, :] = (o * gn_ref[...] * jax.nn.sigmoid(og[rows])).astype(y_ref.dtype)


def _mlstm(xb, w, conv_w, gbias, gn, tables):
    B, T, _ = xb.shape
    const = lambda shape: pl.BlockSpec(shape, lambda b, t: (0,) * len(shape))
    return pl.pallas_call(
        _mlstm_kernel,
        grid=(B, T // TOKEN_BLOCK),
        in_specs=[
            pl.BlockSpec((1, TOKEN_BLOCK, D_MODEL), lambda b, t: (b, t, 0)),
            const((D_MODEL, MLSTM_W)), const((4, 512)), const((1, 128)), const((1, GROUP)),
            const((TOKEN_BLOCK, TOKEN_BLOCK)), const((4 * MLSTM_L, 256)),
        ],
        out_specs=pl.BlockSpec((1, TOKEN_BLOCK, GROUP), lambda b, t: (b, t, 0)),
        out_shape=jax.ShapeDtypeStruct((B, T, GROUP), BF16),
        scratch_shapes=[pltpu.VMEM((TOKEN_BLOCK + 8, 512), F32), pltpu.VMEM((256, GROUP), F32),
                        pltpu.VMEM((1, 256), F32), pltpu.VMEM((8, 128), F32)],
        compiler_params=pltpu.CompilerParams(dimension_semantics=("arbitrary", "arbitrary"),
                                             vmem_limit_bytes=VMEM_LIMIT),
        name="mlstm",
    )(xb, w, conv_w, gbias, gn, *[jnp.asarray(t) for t in tables])


ROW_BLOCK = 512


def _layer_norm(z, g, b):
    mu = jnp.mean(z, axis=-1, keepdims=True)
    d = z - mu
    var = jnp.mean(d * d, axis=-1, keepdims=True)
    return d * lax.rsqrt(var + NORM_EPS) * g + b


def _outproj_kernel(y0_ref, y1_ref, y2_ref, y3_ref, x_ref, w_ref, g_ref, b_ref, rw_ref, rb_ref,
                    x1_ref, x1b_ref, idx_ref, gate_ref):
    acc = jnp.dot(y0_ref[...], w_ref[0:512, :], preferred_element_type=F32)
    acc += jnp.dot(y1_ref[...], w_ref[512:1024, :], preferred_element_type=F32)
    acc += jnp.dot(y2_ref[...], w_ref[1024:1536, :], preferred_element_type=F32)
    acc += jnp.dot(y3_ref[...], w_ref[1536:2048, :], preferred_element_type=F32)
    x1 = _layer_norm(DN_ALPHA * x_ref[...] + acc, g_ref[...], b_ref[...])
    x1_ref[...] = x1
    x1b_ref[...] = x1.astype(BF16)
    logits = _dot_hi(x1, rw_ref[...]) + rb_ref[...]
    lane = lax.broadcasted_iota(jnp.int32, logits.shape, 1)
    logits = jnp.where(lane < N_EXPERTS, logits, -jnp.inf)
    idx_out = jnp.zeros(logits.shape, jnp.int32)
    gate_out = jnp.zeros(logits.shape, F32)
    top = None
    for kth in range(TOP_K):
        m = jnp.max(logits, axis=-1, keepdims=True)
        sel = jnp.min(jnp.where(logits == m, lane, LANE), axis=-1, keepdims=True)
        if kth == 0:
            top = m
        idx_out = jnp.where(lane == kth, sel, idx_out)
        gate_out = jnp.where(lane == kth, jnp.exp(m - top), gate_out)
        logits = jnp.where(lane == sel, -jnp.inf, logits)
    gate_out = gate_out / jnp.sum(gate_out, axis=-1, keepdims=True)
    idx_ref[...] = idx_out
    gate_ref[...] = gate_out


def _outproj(ys, x, w_out, g, b, rw, rb):
    N = x.shape[0]
    rb_ = ROW_BLOCK
    const = lambda shape: pl.BlockSpec(shape, lambda i: (0,) * len(shape))
    rows = lambda width: pl.BlockSpec((rb_, width), lambda i: (i, 0))
    return pl.pallas_call(
        _outproj_kernel,
        grid=(N // rb_,),
        in_specs=[rows(GROUP)] * 4 + [rows(D_MODEL), const((D_MODEL, D_MODEL)), const((1, D_MODEL)), const((1, D_MODEL)),
                                      const((D_MODEL, LANE)), const((1, LANE))],
        out_specs=[rows(D_MODEL), rows(D_MODEL), rows(LANE), rows(LANE)],
        out_shape=[jax.ShapeDtypeStruct((N, D_MODEL), F32), jax.ShapeDtypeStruct((N, D_MODEL), BF16),
                   jax.ShapeDtypeStruct((N, LANE), jnp.int32), jax.ShapeDtypeStruct((N, LANE), F32)],
        compiler_params=pltpu.CompilerParams(dimension_semantics=("arbitrary",), vmem_limit_bytes=VMEM_LIMIT),
        name="outproj_norm_router",
    )(*ys, x, w_out, g, b, rw, rb)


MOE_BLOCK = 512
FF_TILE = 512


def _ffn_kernel(be_ref, nb_ref, xs_ref, wg_ref, wl_ref, bg_ref, bl_ref, wd_ref, bd_ref, gate_ref, o_ref, acc_ref):
    i, f = pl.program_id(0), pl.program_id(1)

    @pl.when(f == 0)
    def _():
        acc_ref[...] = jnp.zeros_like(acc_ref)

    @pl.when(i < nb_ref[0])
    def _():
        xs = xs_ref[...]
        glu = jnp.dot(xs, wg_ref[...], preferred_element_type=F32) + bg_ref[...]
        lin = jnp.dot(xs, wl_ref[...], preferred_element_type=F32) + bl_ref[...]
        glu = jnp.minimum(glu, SWIGLU_LIMIT)
        lin = jnp.clip(lin, -SWIGLU_LIMIT, SWIGLU_LIMIT)
        act = glu * jax.nn.sigmoid(SWIGLU_ALPHA * glu) * (lin + 1.0)
        acc_ref[...] += jnp.dot(act.astype(BF16), wd_ref[...], preferred_element_type=F32)

    @pl.when(f == pl.num_programs(1) - 1)
    def _():
        o_ref[...] = ((acc_ref[...] + bd_ref[...]) * gate_ref[...]).astype(o_ref.dtype)


def _expert_ffn(xs, block_exp, n_used, w_gu, b_gu, w_dn, b_dn, slot_gate):
    n_slots = xs.shape[0]
    nb, nf = n_slots // MOE_BLOCK, D_FF // FF_TILE
    grid_spec = pltpu.PrefetchScalarGridSpec(
        num_scalar_prefetch=2,
        grid=(nb, nf),
        in_specs=[
            pl.BlockSpec((MOE_BLOCK, D_MODEL), lambda i, f, be, n: (i, 0)),
            pl.BlockSpec((None, D_MODEL, FF_TILE), lambda i, f, be, n: (be[i], 0, f)),
            pl.BlockSpec((None, D_MODEL, FF_TILE), lambda i, f, be, n: (be[i], 0, nf + f)),
            pl.BlockSpec((None, 1, FF_TILE), lambda i, f, be, n: (be[i], 0, f)),
            pl.BlockSpec((None, 1, FF_TILE), lambda i, f, be, n: (be[i], 0, nf + f)),
            pl.BlockSpec((None, FF_TILE, D_MODEL), lambda i, f, be, n: (be[i], f, 0)),
            pl.BlockSpec((None, 1, D_MODEL), lambda i, f, be, n: (be[i], 0, 0)),
            pl.BlockSpec((MOE_BLOCK, 1), lambda i, f, be, n: (i, 0)),
        ],
        out_specs=pl.BlockSpec((MOE_BLOCK, D_MODEL), lambda i, f, be, n: (i, 0)),
        scratch_shapes=[pltpu.VMEM((MOE_BLOCK, D_MODEL), F32)],
    )
    return pl.pallas_call(
        _ffn_kernel,
        grid_spec=grid_spec,
        out_shape=jax.ShapeDtypeStruct((n_slots, D_MODEL), BF16),
        compiler_params=pltpu.CompilerParams(dimension_semantics=("arbitrary", "arbitrary"),
                                             vmem_limit_bytes=VMEM_LIMIT),
        name="expert_ffn",
    )(block_exp, n_used, xs, w_gu, w_gu, b_gu, b_gu, w_dn, b_dn, slot_gate)


def _combine_kernel(x_ref, ys_ref, g_ref, b_ref, o_ref, ob_ref):
    y = ys_ref[:, 0:D_MODEL].astype(F32)
    for kth in range(1, TOP_K):
        y = y + ys_ref[:, kth * D_MODEL:(kth + 1) * D_MODEL].astype(F32)
    x2 = _layer_norm(DN_ALPHA * x_ref[...] + y, g_ref[...], b_ref[...])
    o_ref[...] = x2
    ob_ref[...] = x2.astype(BF16)


def _combine(x1, ysg, g, b):
    N = x1.shape[0]
    rb_ = ROW_BLOCK
    const = lambda shape: pl.BlockSpec(shape, lambda i: (0,) * len(shape))
    return pl.pallas_call(
        _combine_kernel,
        grid=(N // rb_,),
        in_specs=[pl.BlockSpec((rb_, D_MODEL), lambda i: (i, 0)), pl.BlockSpec((rb_, TOP_K * D_MODEL), lambda i: (i, 0)),
                  const((1, D_MODEL)), const((1, D_MODEL))],
        out_specs=[pl.BlockSpec((rb_, D_MODEL), lambda i: (i, 0))] * 2,
        out_shape=[jax.ShapeDtypeStruct((N, D_MODEL), F32), jax.ShapeDtypeStruct((N, D_MODEL), BF16)],
        compiler_params=pltpu.CompilerParams(dimension_semantics=("arbitrary",), vmem_limit_bytes=VMEM_LIMIT),
        name="combine_norm",
    )(x1, ysg, g, b)


def _route(top_idx, gate):
    n_tok = top_idx.shape[0]
    n_assign = n_tok * TOP_K
    flat_e = top_idx.reshape(-1)
    flat_gate = gate.reshape(-1)
    order = jnp.argsort(flat_e)
    se = flat_e[order]
    counts = jnp.bincount(flat_e, length=N_EXPERTS)
    starts = jnp.cumsum(counts) - counts
    pcounts = (counts + MOE_BLOCK - 1) // MOE_BLOCK * MOE_BLOCK
    pends = jnp.cumsum(pcounts)
    pstarts = pends - pcounts
    dest = (pstarts[se] + jnp.arange(n_assign, dtype=jnp.int32) - starts[se]).astype(jnp.int32)
    n_slots = -(-n_assign // MOE_BLOCK) * MOE_BLOCK + N_EXPERTS * MOE_BLOCK
    n_blocks = n_slots // MOE_BLOCK
    slot_tok = jnp.zeros((n_slots,), jnp.int32).at[dest].set((order // TOP_K).astype(jnp.int32))
    slot_gate = jnp.zeros((n_slots,), F32).at[dest].set(flat_gate[order])
    pos = jnp.zeros((n_assign,), jnp.int32).at[order].set(dest)
    block_start = jnp.arange(n_blocks, dtype=jnp.int32) * MOE_BLOCK
    block_exp = jnp.minimum(jnp.searchsorted(pends, block_start, side='right'), N_EXPERTS - 1).astype(jnp.int32)
    n_used = (pends[-1] // MOE_BLOCK).astype(jnp.int32).reshape(1)
    return slot_tok, slot_gate, pos, block_exp, n_used


def _pad_rows(w, row0, total):
    return jnp.zeros((total, w.shape[1]), F32).at[row0:row0 + w.shape[0]].set(w)


def kernel(x, w_in, rwkv_mu, rwkv_w0, rwkv_w2, rwkv_a0, rwkv_a2, rwkv_g2, rwkv_k_k, rwkv_k_a, rwkv_r_k, gla_g2, gla_gb, mlstm_conv, mlstm_ib, mlstm_fb, head_norm_g, w_out, ln1_g, ln1_b, router_w, router_b, w_gu, b_gu, w_dn, b_dn, ln2_g, ln2_b):
    B, T, D = x.shape
    N = B * T
    depth = w_in.shape[0]
    ret_tab, rwkv_tab, gla_tab, mlstm_tab = _ret_tables(), _rwkv_tables(), _gla_tables(), _mlstm_tables()
    pos_t = jnp.arange(T, dtype=F32)
    inv = ROPE_BASE ** (-jnp.arange(32, dtype=F32) / 32)
    ang = pos_t[:, None] * inv[None, :]
    cos = jnp.tile(jnp.cos(ang), (1, 4))
    sin = jnp.tile(jnp.sin(ang), (1, 4))
    row = lambda v: v.reshape(1, -1)

    xf = x.reshape(N, D)
    xb = x.astype(BF16)
    for l in range(depth):
        w_ext = jnp.concatenate([w_in[l], jnp.zeros((D, 1), F32)], axis=1)
        seg = lambda cols: jnp.take(w_ext, jnp.asarray(cols), axis=1).astype(BF16)
        gn = head_norm_g[l]
        y_ret = _retention(xb, seg(_ret_cols()), cos, sin, row(gn[0:512]), ret_tab)
        mu = jnp.concatenate([rwkv_mu[l], jnp.zeros((96,), F32)])
        y_rwkv = _rwkv(xb, seg(_rwkv_cols()), row(mu), row(rwkv_w0[l]), _pad_rows(rwkv_w2[l], 0, 256),
                       row(rwkv_a0[l]), _pad_rows(rwkv_a2[l], 32, 256), _pad_rows(rwkv_g2[l], 64, 256),
                       row(rwkv_k_k[l]), row(rwkv_k_a[l]), row(rwkv_r_k[l]), row(gn[512:1024]), rwkv_tab)
        y_gla = _gla(xb, seg(_gla_cols()), _pad_rows(gla_g2[l], 0, 128), row(gla_gb[l]), row(gn[1024:1536]), gla_tab)
        gbias = jnp.concatenate([mlstm_ib[l], mlstm_fb[l], jnp.zeros((120,), F32)])
        y_mlstm = _mlstm(xb, seg(_mlstm_cols()), mlstm_conv[l], row(gbias), row(gn[1536:2048]), mlstm_tab)
        ys = [y.reshape(N, GROUP) for y in (y_ret, y_rwkv, y_gla, y_mlstm)]
        rw = jnp.zeros((D, LANE), F32).at[:, :N_EXPERTS].set(router_w[l])
        rb = jnp.zeros((1, LANE), F32).at[0, :N_EXPERTS].set(router_b[l])
        x1, x1b, idx, gate = _outproj(ys, xf, w_out[l].astype(BF16), row(ln1_g[l]), row(ln1_b[l]), rw, rb)
        slot_tok, slot_gate, pos, block_exp, n_used = _route(idx[:, :TOP_K], gate[:, :TOP_K])
        xs = jnp.take(x1b, slot_tok, axis=0)
        ys_moe = _expert_ffn(xs, block_exp, n_used, w_gu[l].astype(BF16), b_gu[l].reshape(N_EXPERTS, 1, -1),
                             w_dn[l].astype(BF16), b_dn[l].reshape(N_EXPERTS, 1, -1), slot_gate.reshape(-1, 1))
        ysg = jnp.take(ys_moe, pos, axis=0).reshape(N, TOP_K * D)
        xf, xb2 = _combine(x1, ysg, row(ln2_g[l]), row(ln2_b[l]))
        xb = xb2.reshape(B, T, D)
    return xf.reshape(B, T, D)
```

```python
import functools
import math

import numpy as np
import jax
import jax.numpy as jnp
from jax import lax
from jax.experimental import pallas as pl
from jax.experimental.pallas import tpu as pltpu

F32 = jnp.float32
BF16 = jnp.bfloat16
HI = lax.Precision.HIGHEST

D_MODEL = 2048
GROUP = 512
N_EXPERTS = 32
TOP_K = 4
D_FF = 2048
DEPTH = 2
NORM_EPS = 1e-5
DN_ALPHA = (2 * DEPTH) ** 0.25
ROPE_BASE = 10000.0
GLA_TAU = 16.0
SWIGLU_ALPHA = 1.702
SWIGLU_LIMIT = 7.0

LANE = 128
TOKEN_BLOCK = 256
VMEM_LIMIT = 56 * 1024 * 1024

RET_OFF, RWKV_OFF, GLA_OFF, MLSTM_OFF, D_IN = 0, 1536, 3232, 4784, 6328
RET_W, RWKV_W, GLA_W, MLSTM_W = 1536, 1792, 1664, 1664


def _ret_cols():
    def half_split(base):
        lo = [base + h * 64 + j for h in range(4) for j in range(32)]
        hi = [base + h * 64 + 32 + j for h in range(4) for j in range(32)]
        return lo + hi
    cols = half_split(RET_OFF) + half_split(RET_OFF + 256)
    cols += list(range(RET_OFF + 512, RET_OFF + 1536))
    return np.asarray(cols, np.int32)


def _rwkv_cols():
    cols = list(range(RWKV_OFF, RWKV_OFF + 1696)) + [D_IN] * 96
    return np.asarray(cols, np.int32)


def _gla_cols():
    o = GLA_OFF
    cols = list(range(o, o + 1024)) + list(range(o + 1040, o + 1552)) + list(range(o + 1024, o + 1040)) + [D_IN] * 112
    return np.asarray(cols, np.int32)


def _mlstm_cols():
    o = MLSTM_OFF
    cols = list(range(o, o + 1024)) + list(range(o + 1032, o + 1544)) + list(range(o + 1024, o + 1032)) + [D_IN] * 120
    return np.asarray(cols, np.int32)


def _dot(a, b):
    return jnp.dot(a.astype(BF16), b.astype(BF16), preferred_element_type=F32)


def _dot_nt(a, b):
    return lax.dot_general(a.astype(BF16), b.astype(BF16), (((1,), (1,)), ((), ())), preferred_element_type=F32)


def _dot_tn(a, b):
    return lax.dot_general(a.astype(BF16), b.astype(BF16), (((0,), (0,)), ((), ())), preferred_element_type=F32)


def _split(a, terms):
    pieces, rem = [], a
    for t in range(terms):
        piece = rem.astype(BF16)
        pieces.append(piece)
        if t + 1 < terms:
            rem = rem - piece.astype(F32)
    return pieces


def _dot_sel(a, sel, terms):
    out = None
    for piece in _split(a, terms):
        d = jnp.dot(piece, sel, preferred_element_type=F32)
        out = d if out is None else out + d
    return out


def _sel_dot(sel, b, terms):
    out = None
    for piece in _split(b, terms):
        d = jnp.dot(sel, piece, preferred_element_type=F32)
        out = d if out is None else out + d
    return out


def _silu(x):
    return x * jax.nn.sigmoid(x)


def _head_norm_lanes(o, width):
    parts = []
    for h in range(o.shape[1] // width):
        oh = o[:, h * width:(h + 1) * width]
        mu = jnp.mean(oh, axis=-1, keepdims=True)
        d = oh - mu
        var = jnp.mean(d * d, axis=-1, keepdims=True)
        parts.append(d * lax.rsqrt(var + NORM_EPS))
    return jnp.concatenate(parts, axis=1)


RET_L = 128


def _ret_tables():
    lg = np.log(1.0 - 2.0 ** (-5.0 - np.arange(4, dtype=np.float64)))
    i = np.arange(RET_L, dtype=np.float64)
    lane = np.arange(256)
    head_of_lane = (lane % 128) // 32
    ms = np.zeros((4 * RET_L, 256), np.float32)
    mq = np.zeros((4 * RET_L, 256), np.float32)
    dk = np.zeros((4 * RET_L, 256), np.float32)
    di = np.zeros((4 * RET_L, RET_L), np.float32)
    for h in range(4):
        m = (head_of_lane == h).astype(np.float64)
        ms[h * RET_L:(h + 1) * RET_L] = m[None, :]
        mq[h * RET_L:(h + 1) * RET_L] = np.exp((i + 1.0) * lg[h])[:, None] * m[None, :]
        dk[h * RET_L:(h + 1) * RET_L] = np.exp((RET_L - 1.0 - i) * lg[h])[:, None]
        rel = i[:, None] - i[None, :]
        di[h * RET_L:(h + 1) * RET_L] = np.where(rel >= 0, np.exp(np.maximum(rel, 0.0) * lg[h]), 0.0)
    dchunk = np.exp(RET_L * lg).astype(np.float32)
    return ms, mq, dk, di, dchunk


def _ret_kernel(x_ref, w_ref, cos_ref, sin_ref, ms_ref, mq_ref, dk_ref, di_ref, g_ref, y_ref, s_ref, *, dchunk):
    @pl.when(pl.program_id(1) == 0)
    def _():
        s_ref[...] = jnp.zeros_like(s_ref)

    p = jnp.dot(x_ref[0], w_ref[...], preferred_element_type=F32)
    L = RET_L
    for c in range(TOKEN_BLOCK // L):
        sl = slice(c * L, (c + 1) * L)
        q, k = p[sl, 0:256], p[sl, 256:512]
        v, gate = p[sl, 512:1024], p[sl, 1024:1536]
        cs, sn = cos_ref[sl, :], sin_ref[sl, :]

        def rope(t):
            t1, t2 = t[:, :128], t[:, 128:]
            return jnp.concatenate([t1 * cs - t2 * sn, t1 * sn + t2 * cs], axis=1)

        qr = rope(q) * (64.0 ** -0.5)
        kr = rope(k)
        q4 = jnp.concatenate([qr] * 4, axis=0)
        scores = _dot_nt(q4 * ms_ref[...], kr) * di_ref[...]
        qd = q4 * mq_ref[...]
        k4 = jnp.concatenate([kr] * 4, axis=0) * dk_ref[...]
        outs = []
        for h in range(4):
            hb = slice(h * L, (h + 1) * L)
            vb = slice(h * 128, (h + 1) * 128)
            vh = v[:, vb]
            sh = s_ref[:, vb]
            o = _dot(scores[hb], vh) + _dot(qd[hb], sh)
            outs.append(o)
            s_ref[:, vb] = dchunk[h] * sh + _dot_tn(k4[hb], vh)
        o = _head_norm_lanes(jnp.concatenate(outs, axis=1), 128)
        y_ref[0, sl, :] = (o * g_ref[...] * _silu(gate)).astype(y_ref.dtype)


def _retention(xb, w, cos, sin, g, tables):
    B, T, _ = xb.shape
    ms, mq, dk, di, dchunk = tables
    const = lambda shape: pl.BlockSpec(shape, lambda b, t: (0,) * len(shape))
    return pl.pallas_call(
        functools.partial(_ret_kernel, dchunk=tuple(float(d) for d in dchunk)),
        grid=(B, T // TOKEN_BLOCK),
        in_specs=[
            pl.BlockSpec((1, TOKEN_BLOCK, D_MODEL), lambda b, t: (b, t, 0)),
            const((D_MODEL, RET_W)),
            pl.BlockSpec((TOKEN_BLOCK, 128), lambda b, t: (t, 0)),
            pl.BlockSpec((TOKEN_BLOCK, 128), lambda b, t: (t, 0)),
            const((4 * RET_L, 256)), const((4 * RET_L, 256)), const((4 * RET_L, 256)), const((4 * RET_L, RET_L)),
            const((1, GROUP)),
        ],
        out_specs=pl.BlockSpec((1, TOKEN_BLOCK, GROUP), lambda b, t: (b, t, 0)),
        out_shape=jax.ShapeDtypeStruct((B, T, GROUP), BF16),
        scratch_shapes=[pltpu.VMEM((256, GROUP), F32)],
        compiler_params=pltpu.CompilerParams(dimension_semantics=("arbitrary", "arbitrary"),
                                             vmem_limit_bytes=VMEM_LIMIT),
        name="retention",
    )(xb, w, cos, sin, jnp.asarray(ms), jnp.asarray(mq), jnp.asarray(dk), jnp.asarray(di), g)


RWKV_L = 64


def _rwkv_tables():
    lane = np.arange(256)
    row = np.arange(256)
    ms4 = (lane[None, :] // 64 == row[:, None] // 64).astype(np.float32)
    same = (row[:, None] // 64 == row[None, :] // 64)
    tt, ii = row[:, None] % 64, row[None, :] % 64
    strict = (same & (tt > ii)).astype(np.float32)
    incl = (same & (tt >= ii)).astype(np.float32)
    tri = incl.copy()
    r5 = np.arange(512)
    bd = (r5[:, None] // 64 == r5[None, :] // 64).astype(np.float32)
    return ms4, strict, incl, tri, bd


def _rwkv_kernel(x_ref, w_ref, mu_ref, w0_ref, w2_ref, a0_ref, a2_ref, g2_ref, kk_ref, ka_ref, rk_ref, gn_ref,
                 ms4_ref, strict_ref, incl_ref, tri_ref, bd_ref, y_ref, p_scr, s_scr, o_scr):
    TC, L = TOKEN_BLOCK, RWKV_L

    @pl.when(pl.program_id(1) == 0)
    def _():
        s_scr[...] = jnp.zeros_like(s_scr)
        p_scr[pl.ds(0, 8), :] = jnp.zeros((8, RWKV_W), F32)

    p = jnp.dot(x_ref[0], w_ref[...], preferred_element_type=F32)
    p_scr[pl.ds(8, TC), :] = p
    prev = p_scr[pl.ds(7, TC), :]
    p_scr[pl.ds(0, 8), :] = p[TC - 8:, :]
    cols = p + (prev - p) * mu_ref[...]
    r, k, v = cols[:, 0:512], cols[:, 512:1024], cols[:, 1024:1536]
    tail = cols[:, 1536:1792]
    w = -jax.nn.softplus(-(w0_ref[...] + _dot(jnp.tanh(tail), w2_ref[...]))) - 0.5
    logdec = -jnp.exp(w)
    a = jax.nn.sigmoid(a0_ref[...] + _dot(tail, a2_ref[...]))
    g = _dot(jax.nn.sigmoid(tail), g2_ref[...])
    kk = k * kk_ref[...]
    bd = bd_ref[...]
    kk = kk / jnp.maximum(jnp.sqrt(_dot_sel(kk * kk, bd, 2)), 1e-12)
    k = k * (1.0 + (a - 1.0) * ka_ref[...])
    bonus = _dot_sel(r * k * rk_ref[...], bd, 1) * v
    cum = _sel_dot(tri_ref[...], logdec, 3)
    beta = kk * a
    ms4 = ms4_ref[...]
    strict, incl = strict_ref[...], incl_ref[...]

    def stack(t):
        return jnp.concatenate([t] * 4, axis=0) * ms4

    for c in range(TC // L):
        rows = slice(c * L, (c + 1) * L)
        for gi in range(2):
            lanes = slice(gi * 256, (gi + 1) * 256)
            cm, lw = cum[rows, lanes], logdec[rows, lanes]
            cl = cm[L - 1:L, :]
            gam, gam_ex, igam, gam_r = jnp.exp(cm), jnp.exp(cm - lw), jnp.exp(-cm), jnp.exp(cl - cm)
            kk_c, r_c, k_c, v_c, b_c = kk[rows, lanes], r[rows, lanes], k[rows, lanes], v[rows, lanes], beta[rows, lanes]
            x1, x2 = stack(-kk_c * gam_ex), stack(r_c * gam)
            y1, y2 = stack(b_c * igam), stack(k_c * igam)
            z1, z2 = stack(b_c * gam_r), stack(k_c * gam_r)
            sv = stack(v_c)
            s0 = s_scr[gi]
            ab = _dot_nt(jnp.concatenate([x1, x2], axis=0), jnp.concatenate([y1, y2], axis=0))
            a_b, a_k = ab[0:256, 0:256] * strict, ab[0:256, 256:512] * strict
            b_b, b_k = ab[256:512, 0:256] * incl, ab[256:512, 256:512] * incl
            u = _dot(a_k, sv) + _dot_nt(x1, s0)
            pw = a_b
            for it in range(6):
                u = u + _dot(pw, u)
                if it < 5:
                    pw = _dot(pw, pw)
            yst = _dot_nt(x2, s0) + _dot(b_b, u) + _dot(b_k, sv)
            o_scr[rows, lanes] = yst[0:64] + yst[64:128] + yst[128:192] + yst[192:256]
            s_scr[gi] = s0 * jnp.exp(cl) + _dot_tn(u, z1) + _dot_tn(sv, z2)

    y = o_scr[...]
    mean = _dot_sel(y, bd, 2) * (1.0 / 64.0)
    d = y - mean
    var = _dot_sel(d * d, bd, 2) * (1.0 / 64.0)
    yn = d * lax.rsqrt(var + NORM_EPS) * gn_ref[...]
    y_ref[0] = ((yn + bonus) * g).astype(y_ref.dtype)


def _rwkv(xb, w, mu, w0, w2p, a0, a2p, g2p, k_k, k_a, r_k, gn, tables):
    B, T, _ = xb.shape
    const = lambda shape: pl.BlockSpec(shape, lambda b, t: (0,) * len(shape))
    row = lambda n: const((1, n))
    return pl.pallas_call(
        _rwkv_kernel,
        grid=(B, T // TOKEN_BLOCK),
        in_specs=[
            pl.BlockSpec((1, TOKEN_BLOCK, D_MODEL), lambda b, t: (b, t, 0)),
            const((D_MODEL, RWKV_W)), row(RWKV_W), row(GROUP), const((256, GROUP)), row(GROUP), const((256, GROUP)),
            const((256, GROUP)), row(GROUP), row(GROUP), row(GROUP), row(GROUP),
            const((256, 256)), const((256, 256)), const((256, 256)), const((256, 256)), const((512, 512)),
        ],
        out_specs=pl.BlockSpec((1, TOKEN_BLOCK, GROUP), lambda b, t: (b, t, 0)),
        out_shape=jax.ShapeDtypeStruct((B, T, GROUP), BF16),
        scratch_shapes=[pltpu.VMEM((TOKEN_BLOCK + 8, RWKV_W), F32), pltpu.VMEM((2, 256, 256), F32),
                        pltpu.VMEM((TOKEN_BLOCK, GROUP), F32)],
        compiler_params=pltpu.CompilerParams(dimension_semantics=("arbitrary", "arbitrary"),
                                             vmem_limit_bytes=VMEM_LIMIT),
        name="rwkv7",
    )(xb, w, mu, w0, w2p, a0, a2p, g2p, k_k, k_a, r_k, gn,
      *[jnp.asarray(t, dt) for t, dt in zip(tables, (F32, F32, F32, BF16, BF16))])


GLA_SUB = 16


def _gla_tables():
    r = np.arange(256)
    tri16 = ((r[:, None] // 16 == r[None, :] // 16) & (r[:, None] >= r[None, :])).astype(np.float32)
    sel = (r[:, None] // 64 == np.arange(512)[None, :] // 128).astype(np.float32)
    ms = (np.arange(256)[None, :] // 64 == np.arange(64)[:, None] // 16).astype(np.float32)
    return tri16, sel, ms


def _gla_kernel(x_ref, w_ref, g2_ref, gb_ref, gn_ref, tri_ref, sel_ref, ms_ref, y_ref, st_scr):
    TC, SUB = TOKEN_BLOCK, GLA_SUB

    @pl.when(pl.program_id(1) == 0)
    def _():
        st_scr[...] = jnp.zeros_like(st_scr)

    p = jnp.dot(x_ref[0], w_ref[...], preferred_element_type=F32)
    gate = p[:, 1024:1536]
    la = jax.nn.log_sigmoid(_dot(p[:, 1536:1664], g2_ref[...]) + gb_ref[...]) * (1.0 / GLA_TAU)
    q, k, v = p[:, 0:256] * (64.0 ** -0.5), p[:, 256:512], p[:, 512:1024]
    b = _sel_dot(tri_ref[...], la, 3)
    sel = sel_ref[...]
    ms = ms_ref[...]
    row = lax.broadcasted_iota(jnp.int32, (SUB, 1), 0)
    outs = []
    for s in range(TC // SUB):
        rows = slice(s * SUB, (s + 1) * SUB)
        qb, kb, vb, bb = q[rows], k[rows], v[rows], b[rows]
        bl = bb[SUB - 1:SUB, :]
        qe = jnp.concatenate([qb * jnp.exp(bb)] * 4, axis=0) * ms
        o_inter = jnp.concatenate(
            [_dot_nt(qe[h * SUB:(h + 1) * SUB], st_scr[h * 128:(h + 1) * 128, :]) for h in range(4)], axis=1)
        ts = []
        for j in range(SUB):
            diff = jnp.where(row >= j, bb - bb[j:j + 1, :], -1e30)
            ts.append(qb * kb[j:j + 1, :] * jnp.exp(diff))
        rr = jnp.dot(jnp.concatenate(ts, axis=0).astype(BF16), sel, preferred_element_type=F32)
        o_intra = rr[0:SUB] * vb[0:1, :]
        for j in range(1, SUB):
            o_intra = o_intra + rr[j * SUB:(j + 1) * SUB] * vb[j:j + 1, :]
        outs.append(o_intra + o_inter)
        st_scr[...] = st_scr[...] * jnp.exp(bl) + _dot_tn(vb, kb * jnp.exp(bl - bb))
    o = _head_norm_lanes(jnp.concatenate(outs, axis=0), 128)
    y_ref[0] = (o * gn_ref[...] * _silu(gate)).astype(y_ref.dtype)


def _gla(xb, w, g2p, gb, gn, tables):
    B, T, _ = xb.shape
    const = lambda shape: pl.BlockSpec(shape, lambda b, t: (0,) * len(shape))
    return pl.pallas_call(
        _gla_kernel,
        grid=(B, T // TOKEN_BLOCK),
        in_specs=[
            pl.BlockSpec((1, TOKEN_BLOCK, D_MODEL), lambda b, t: (b, t, 0)),
            const((D_MODEL, GLA_W)), const((128, 256)), const((1, 256)), const((1, GROUP)),
            const((256, 256)), const((256, 512)), const((64, 256)),
        ],
        out_specs=pl.BlockSpec((1, TOKEN_BLOCK, GROUP), lambda b, t: (b, t, 0)),
        out_shape=jax.ShapeDtypeStruct((B, T, GROUP), BF16),
        scratch_shapes=[pltpu.VMEM((GROUP, 256), F32)],
        compiler_params=pltpu.CompilerParams(dimension_semantics=("arbitrary", "arbitrary"),
                                             vmem_limit_bytes=VMEM_LIMIT),
        name="gla",
    )(xb, w, g2p, gb, gn, *[jnp.asarray(t, dt) for t, dt in zip(tables, (BF16, BF16, F32))])


MLSTM_L = 128


def _mlstm_tables():
    r = np.arange(TOKEN_BLOCK)
    tri = ((r[:, None] // MLSTM_L == r[None, :] // MLSTM_L) & (r[:, None] >= r[None, :])).astype(np.float32)
    ms = (np.arange(256)[None, :] // 64 == np.arange(4 * MLSTM_L)[:, None] // MLSTM_L).astype(np.float32)
    return tri, ms


def _mlstm_kernel(x_ref, w_ref, conv_ref, gbias_ref, gn_ref, tri_ref, ms_ref, y_ref,
                  qk_scr, c_scr, n_scr, m_scr):
    TC, L = TOKEN_BLOCK, MLSTM_L

    @pl.when(pl.program_id(1) == 0)
    def _():
        c_scr[...] = jnp.zeros_like(c_scr)
        n_scr[...] = jnp.zeros_like(n_scr)
        m_scr[...] = jnp.zeros_like(m_scr)
        qk_scr[pl.ds(0, 8), :] = jnp.zeros((8, 512), F32)

    p = jnp.dot(x_ref[0], w_ref[...], preferred_element_type=F32)
    qk_scr[pl.ds(8, TC), :] = p[:, 0:512]
    conv = p[:, 0:512] * conv_ref[3:4, :]
    for j in range(3):
        conv = conv + qk_scr[pl.ds(5 + j, TC), :] * conv_ref[j:j + 1, :]
    qk_scr[pl.ds(0, 8), :] = p[TC - 8:, 0:512]
    qk = _silu(conv)
    q, k = qk[:, 0:256], qk[:, 256:512] * (64.0 ** -0.5)
    v, og = p[:, 512:1024], p[:, 1024:1536]
    tail = p[:, 1536:1664] + gbias_ref[...]
    lane = lax.broadcasted_iota(jnp.int32, (1, 128), 1)
    logf = jnp.where((lane >= 4) & (lane < 8), jax.nn.log_sigmoid(tail), 0.0)
    cumf = _sel_dot(tri_ref[...], logf, 3)
    gcol = jnp.where(lane < 4, tail, cumf)
    eye = (lax.broadcasted_iota(jnp.int32, (128, 128), 0) == lax.broadcasted_iota(jnp.int32, (128, 128), 1)).astype(BF16)
    ms = ms_ref[...]
    ri = lax.broadcasted_iota(jnp.int32, (L, L), 0)
    ci = lax.broadcasted_iota(jnp.int32, (L, L), 1)
    causal = ri >= ci
    lane256 = lax.broadcasted_iota(jnp.int32, (1, 256), 1)

    for c in range(TC // L):
        rows = slice(c * L, (c + 1) * L)
        gc = gcol[rows, :]
        gr = sum(lax.dot_general(eye, piece, (((1,), (1,)), ((), ())), preferred_element_type=F32)
                 for piece in _split(gc, 3))
        qc, kc, vc = q[rows], k[rows], v[rows]
        q4 = jnp.concatenate([qc] * 4, axis=0) * ms
        scores = _dot_nt(q4, kc)
        qn = jnp.sum(q4 * n_scr[...], axis=1, keepdims=True)
        outs, wrows, n_scale = [], [], jnp.zeros((1, 256), F32)
        for h in range(4):
            hb = slice(h * L, (h + 1) * L)
            vb = slice(h * 128, (h + 1) * 128)
            b_col, i_col = gc[:, 4 + h:5 + h], gc[:, h:h + 1]
            b_row, i_row = gr[4 + h:5 + h, :], gr[h:h + 1, :]
            m_prev = m_scr[0:1, h:h + 1]
            log_d = jnp.where(causal, b_col - b_row + i_row, -jnp.inf)
            m_inter = b_col + m_prev
            m_row = jnp.maximum(jnp.max(log_d, axis=-1, keepdims=True), m_inter)
            dmat = jnp.exp(log_d - m_row)
            inter = jnp.exp(m_inter - m_row)
            s = scores[hb] * dmat
            vh = vc[:, vb]
            ch = c_scr[:, vb]
            num = _dot(s, vh) + inter * _dot(q4[hb], ch)
            den = jnp.sum(s, axis=-1, keepdims=True) + inter * qn[hb]
            outs.append(num / jnp.maximum(jnp.abs(den), jnp.exp(-m_row)))
            m_new = m_row[L - 1:L, :]
            b_last = b_col[L - 1:L, :]
            wj_col = jnp.exp(b_last - b_col + i_col - m_new)
            wj_row = jnp.exp(b_last - b_row + i_row - m_new)
            scale = jnp.exp(b_last + m_prev - m_new)
            c_scr[:, vb] = scale * ch + _dot_tn(kc * wj_col, vh)
            head_lanes = lane256 // 64 == h
            wrows.append(wj_row)
            n_scale = jnp.where(head_lanes, scale, n_scale)
            m_scr[0:1, h:h + 1] = m_new
        wk = _dot(jnp.concatenate(wrows + [jnp.zeros((4, L), F32)], axis=0), kc)
        n_add = jnp.zeros((1, 256), F32)
        for h in range(4):
            n_add = jnp.where(lane256 // 64 == h, wk[h:h + 1, :], n_add)
        n_scr[...] = n_scale * n_scr[...] + n_add
        o = _head_norm_lanes(jnp.concatenate(outs, axis=1), 128)
        y_ref[0, rows, :] = (o * gn_ref[...] * jax.nn.sigmoid(og[rows])).astype(y_ref.dtype)


def _mlstm(xb, w, conv_w, gbias, gn, tables):
    B, T, _ = xb.shape
    const = lambda shape: pl.BlockSpec(shape, lambda b, t: (0,) * len(shape))
    return pl.pallas_call(
        _mlstm_kernel,
        grid=(B, T // TOKEN_BLOCK),
        in_specs=[
            pl.BlockSpec((1, TOKEN_BLOCK, D_MODEL), lambda b, t: (b, t, 0)),
            const((D_MODEL, MLSTM_W)), const((4, 512)), const((1, 128)), const((1, GROUP)),
            const((TOKEN_BLOCK, TOKEN_BLOCK)), const((4 * MLSTM_L, 256)),
        ],
        out_specs=pl.BlockSpec((1, TOKEN_BLOCK, GROUP), lambda b, t: (b, t, 0)),
        out_shape=jax.ShapeDtypeStruct((B, T, GROUP), BF16),
        scratch_shapes=[pltpu.VMEM((TOKEN_BLOCK + 8, 512), F32), pltpu.VMEM((256, GROUP), F32),
                        pltpu.VMEM((1, 256), F32), pltpu.VMEM((8, 128), F32)],
        compiler_params=pltpu.CompilerParams(dimension_semantics=("arbitrary", "arbitrary"),
                                             vmem_limit_bytes=VMEM_LIMIT),
        name="mlstm",
    )(xb, w, conv_w, gbias, gn, *[jnp.asarray(t, dt) for t, dt in zip(tables, (BF16, F32))])


ROW_BLOCK = 512


def _layer_norm(z, g, b):
    mu = jnp.mean(z, axis=-1, keepdims=True)
    d = z - mu
    var = jnp.mean(d * d, axis=-1, keepdims=True)
    return d * lax.rsqrt(var + NORM_EPS) * g + b


def _outproj_kernel(y0_ref, y1_ref, y2_ref, y3_ref, x_ref, w_ref, g_ref, b_ref, rwh_ref, rwl_ref, rb_ref,
                    x1_ref, x1b_ref, idx_ref, gate_ref):
    acc = jnp.dot(y0_ref[...], w_ref[0:512, :], preferred_element_type=F32)
    acc += jnp.dot(y1_ref[...], w_ref[512:1024, :], preferred_element_type=F32)
    acc += jnp.dot(y2_ref[...], w_ref[1024:1536, :], preferred_element_type=F32)
    acc += jnp.dot(y3_ref[...], w_ref[1536:2048, :], preferred_element_type=F32)
    x1 = _layer_norm(DN_ALPHA * x_ref[...] + acc, g_ref[...], b_ref[...])
    x1_ref[...] = x1
    x1b_ref[...] = x1.astype(BF16)
    x_hi, x_lo = _split(x1, 2)
    logits = (jnp.dot(x_hi, rwh_ref[...], preferred_element_type=F32)
              + jnp.dot(x_lo, rwh_ref[...], preferred_element_type=F32)
              + jnp.dot(x_hi, rwl_ref[...], preferred_element_type=F32)) + rb_ref[...]
    lane = lax.broadcasted_iota(jnp.int32, logits.shape, 1)
    logits = jnp.where(lane < N_EXPERTS, logits, -jnp.inf)
    idx_out = jnp.zeros(logits.shape, jnp.int32)
    gate_out = jnp.zeros(logits.shape, F32)
    top = None
    for kth in range(TOP_K):
        m = jnp.max(logits, axis=-1, keepdims=True)
        sel = jnp.min(jnp.where(logits == m, lane, LANE), axis=-1, keepdims=True)
        if kth == 0:
            top = m
        idx_out = jnp.where(lane == kth, sel, idx_out)
        gate_out = jnp.where(lane == kth, jnp.exp(m - top), gate_out)
        logits = jnp.where(lane == sel, -jnp.inf, logits)
    gate_out = gate_out / jnp.sum(gate_out, axis=-1, keepdims=True)
    idx_ref[...] = idx_out
    gate_ref[...] = gate_out


def _outproj(ys, x, w_out, g, b, rw, rb):
    N = x.shape[0]
    rb_ = ROW_BLOCK
    const = lambda shape: pl.BlockSpec(shape, lambda i: (0,) * len(shape))
    rows = lambda width: pl.BlockSpec((rb_, width), lambda i: (i, 0))
    rw_hi = rw.astype(BF16)
    rw_lo = (rw - rw_hi.astype(F32)).astype(BF16)
    return pl.pallas_call(
        _outproj_kernel,
        grid=(N // rb_,),
        in_specs=[rows(GROUP)] * 4 + [rows(D_MODEL), const((D_MODEL, D_MODEL)), const((1, D_MODEL)), const((1, D_MODEL)),
                                      const((D_MODEL, LANE)), const((D_MODEL, LANE)), const((1, LANE))],
        out_specs=[rows(D_MODEL), rows(D_MODEL), rows(LANE), rows(LANE)],
        out_shape=[jax.ShapeDtypeStruct((N, D_MODEL), F32), jax.ShapeDtypeStruct((N, D_MODEL), BF16),
                   jax.ShapeDtypeStruct((N, LANE), jnp.int32), jax.ShapeDtypeStruct((N, LANE), F32)],
        compiler_params=pltpu.CompilerParams(dimension_semantics=("arbitrary",), vmem_limit_bytes=VMEM_LIMIT),
        name="outproj_norm_router",
    )(*ys, x, w_out, g, b, rw_hi, rw_lo, rb)


MOE_BLOCK = 512
FF_TILE = 512


def _ffn_kernel(be_ref, nb_ref, xs_ref, wg_ref, wl_ref, bg_ref, bl_ref, wd_ref, bd_ref, gate_ref, o_ref, acc_ref):
    i, f = pl.program_id(0), pl.program_id(1)

    @pl.when(f == 0)
    def _():
        acc_ref[...] = jnp.zeros_like(acc_ref)

    @pl.when(i < nb_ref[0])
    def _():
        xs = xs_ref[...]
        glu = jnp.dot(xs, wg_ref[...], preferred_element_type=F32) + bg_ref[...]
        lin = jnp.dot(xs, wl_ref[...], preferred_element_type=F32) + bl_ref[...]
        glu = jnp.minimum(glu, SWIGLU_LIMIT)
        lin = jnp.clip(lin, -SWIGLU_LIMIT, SWIGLU_LIMIT)
        act = glu * jax.nn.sigmoid(SWIGLU_ALPHA * glu) * (lin + 1.0)
        acc_ref[...] += jnp.dot(act.astype(BF16), wd_ref[...], preferred_element_type=F32)

    @pl.when(f == pl.num_programs(1) - 1)
    def _():
        o_ref[...] = ((acc_ref[...] + bd_ref[...]) * gate_ref[...]).astype(o_ref.dtype)


def _expert_ffn(layer, xs, block_exp, n_used, w_gu, b_gu, w_dn, b_dn, slot_gate):
    n_slots = xs.shape[0]
    nb, nf = n_slots // MOE_BLOCK, D_FF // FF_TILE
    grid_spec = pltpu.PrefetchScalarGridSpec(
        num_scalar_prefetch=2,
        grid=(nb, nf),
        in_specs=[
            pl.BlockSpec((MOE_BLOCK, D_MODEL), lambda i, f, be, n: (i, 0)),
            pl.BlockSpec((None, None, D_MODEL, FF_TILE), lambda i, f, be, n: (layer, be[i], 0, f)),
            pl.BlockSpec((None, None, D_MODEL, FF_TILE), lambda i, f, be, n: (layer, be[i], 0, nf + f)),
            pl.BlockSpec((None, None, 1, FF_TILE), lambda i, f, be, n: (layer, be[i], 0, f)),
            pl.BlockSpec((None, None, 1, FF_TILE), lambda i, f, be, n: (layer, be[i], 0, nf + f)),
            pl.BlockSpec((None, None, FF_TILE, D_MODEL), lambda i, f, be, n: (layer, be[i], f, 0)),
            pl.BlockSpec((None, None, 1, D_MODEL), lambda i, f, be, n: (layer, be[i], 0, 0)),
            pl.BlockSpec((MOE_BLOCK, 1), lambda i, f, be, n: (i, 0)),
        ],
        out_specs=pl.BlockSpec((MOE_BLOCK, D_MODEL), lambda i, f, be, n: (i, 0)),
        scratch_shapes=[pltpu.VMEM((MOE_BLOCK, D_MODEL), F32)],
    )
    return pl.pallas_call(
        _ffn_kernel,
        grid_spec=grid_spec,
        out_shape=jax.ShapeDtypeStruct((n_slots, D_MODEL), BF16),
        compiler_params=pltpu.CompilerParams(dimension_semantics=("arbitrary", "arbitrary"),
                                             vmem_limit_bytes=VMEM_LIMIT),
        name="expert_ffn",
    )(block_exp, n_used, xs, w_gu, w_gu, b_gu, b_gu, w_dn, b_dn, slot_gate)


def _combine_kernel(x_ref, y0_ref, y1_ref, y2_ref, y3_ref, g_ref, b_ref, o_ref, ob_ref):
    y = (y0_ref[...].astype(F32) + y1_ref[...].astype(F32)) + (y2_ref[...].astype(F32) + y3_ref[...].astype(F32))
    x2 = _layer_norm(DN_ALPHA * x_ref[...] + y, g_ref[...], b_ref[...])
    o_ref[...] = x2
    ob_ref[...] = x2.astype(BF16)


def _combine(x1, ysg, g, b):
    N = x1.shape[0]
    rb_ = ROW_BLOCK
    const = lambda shape: pl.BlockSpec(shape, lambda i: (0,) * len(shape))
    rows = pl.BlockSpec((rb_, D_MODEL), lambda i: (i, 0))
    return pl.pallas_call(
        _combine_kernel,
        grid=(N // rb_,),
        in_specs=[rows] * (1 + TOP_K) + [const((1, D_MODEL)), const((1, D_MODEL))],
        out_specs=[rows] * 2,
        out_shape=[jax.ShapeDtypeStruct((N, D_MODEL), F32), jax.ShapeDtypeStruct((N, D_MODEL), BF16)],
        compiler_params=pltpu.CompilerParams(dimension_semantics=("arbitrary",), vmem_limit_bytes=VMEM_LIMIT),
        name="combine_norm",
    )(x1, *ysg, g, b)


def _route(top_idx, gate):
    n_tok = top_idx.shape[0]
    n_assign = n_tok * TOP_K
    flat_e = top_idx.reshape(-1).astype(jnp.int32)
    flat_gate = gate.reshape(-1)
    iota = jnp.arange(n_assign, dtype=jnp.int32)
    se, order = lax.sort((flat_e, iota), num_keys=1)
    counts = jnp.sum((flat_e[:, None] == jnp.arange(N_EXPERTS, dtype=jnp.int32)[None, :]).astype(jnp.int32), axis=0)
    starts = jnp.cumsum(counts) - counts
    pcounts = (counts + MOE_BLOCK - 1) // MOE_BLOCK * MOE_BLOCK
    pends = jnp.cumsum(pcounts)
    pstarts = pends - pcounts
    dest = (pstarts[se] + iota - starts[se]).astype(jnp.int32)
    n_slots = -(-n_assign // MOE_BLOCK) * MOE_BLOCK + N_EXPERTS * MOE_BLOCK
    n_blocks = n_slots // MOE_BLOCK
    block_start = jnp.arange(n_blocks, dtype=jnp.int32) * MOE_BLOCK
    block_exp = jnp.minimum(jnp.searchsorted(pends, block_start, side='right'), N_EXPERTS - 1).astype(jnp.int32)
    slot = jnp.arange(n_slots, dtype=jnp.int32)
    slot_e = jnp.repeat(block_exp, MOE_BLOCK)
    off = slot - pstarts[slot_e]
    valid = off < counts[slot_e]
    src = order[jnp.clip(starts[slot_e] + off, 0, n_assign - 1)]
    slot_tok = jnp.where(valid, src // TOP_K, 0).astype(jnp.int32)
    slot_gate = jnp.where(valid, flat_gate[src], 0.0)
    _, pos = lax.sort((order, dest), num_keys=1)
    n_used = (pends[-1] // MOE_BLOCK).astype(jnp.int32).reshape(1)
    return slot_tok, slot_gate, pos, block_exp, n_used


def _pad_rows(w, row0, total):
    return jnp.zeros((total, w.shape[1]), F32).at[row0:row0 + w.shape[0]].set(w)


def kernel(x, w_in, rwkv_mu, rwkv_w0, rwkv_w2, rwkv_a0, rwkv_a2, rwkv_g2, rwkv_k_k, rwkv_k_a, rwkv_r_k, gla_g2, gla_gb, mlstm_conv, mlstm_ib, mlstm_fb, head_norm_g, w_out, ln1_g, ln1_b, router_w, router_b, w_gu, b_gu, w_dn, b_dn, ln2_g, ln2_b):
    B, T, D = x.shape
    N = B * T
    depth = w_in.shape[0]
    ret_tab, rwkv_tab, gla_tab, mlstm_tab = _ret_tables(), _rwkv_tables(), _gla_tables(), _mlstm_tables()
    pos_t = jnp.arange(T, dtype=F32)
    inv = ROPE_BASE ** (-jnp.arange(32, dtype=F32) / 32)
    ang = pos_t[:, None] * inv[None, :]
    cos = jnp.tile(jnp.cos(ang), (1, 4))
    sin = jnp.tile(jnp.sin(ang), (1, 4))
    row = lambda v: v.reshape(1, -1)

    w_gu_b, w_dn_b = w_gu.astype(BF16), w_dn.astype(BF16)
    b_gu4 = b_gu.reshape(depth, N_EXPERTS, 1, 2 * D_FF)
    b_dn4 = b_dn.reshape(depth, N_EXPERTS, 1, D_MODEL)
    take_rows = lambda a, i: a.at[i].get(mode='promise_in_bounds')

    xf = x.reshape(N, D)
    xb = x.astype(BF16)
    for l in range(depth):
        w_ext = jnp.concatenate([w_in[l], jnp.zeros((D, 1), F32)], axis=1)
        seg = lambda cols: jnp.take(w_ext, jnp.asarray(cols), axis=1).astype(BF16)
        gn = head_norm_g[l]
        y_ret = _retention(xb, seg(_ret_cols()), cos, sin, row(gn[0:512]), ret_tab)
        mu = jnp.concatenate([rwkv_mu[l], jnp.zeros((96,), F32)])
        y_rwkv = _rwkv(xb, seg(_rwkv_cols()), row(mu), row(rwkv_w0[l]), _pad_rows(rwkv_w2[l], 0, 256),
                       row(rwkv_a0[l]), _pad_rows(rwkv_a2[l], 32, 256), _pad_rows(rwkv_g2[l], 64, 256),
                       row(rwkv_k_k[l]), row(rwkv_k_a[l]), row(rwkv_r_k[l]), row(gn[512:1024]), rwkv_tab)
        y_gla = _gla(xb, seg(_gla_cols()), _pad_rows(gla_g2[l], 0, 128), row(gla_gb[l]), row(gn[1024:1536]), gla_tab)
        gbias = jnp.concatenate([mlstm_ib[l], mlstm_fb[l], jnp.zeros((120,), F32)])
        y_mlstm = _mlstm(xb, seg(_mlstm_cols()), mlstm_conv[l], row(gbias), row(gn[1536:2048]), mlstm_tab)
        ys = [y.reshape(N, GROUP) for y in (y_ret, y_rwkv, y_gla, y_mlstm)]
        rw = jnp.zeros((D, LANE), F32).at[:, :N_EXPERTS].set(router_w[l])
        rb = jnp.zeros((1, LANE), F32).at[0, :N_EXPERTS].set(router_b[l])
        x1, x1b, idx, gate = _outproj(ys, xf, w_out[l].astype(BF16), row(ln1_g[l]), row(ln1_b[l]), rw, rb)
        slot_tok, slot_gate, pos, block_exp, n_used = _route(idx[:, :TOP_K], gate[:, :TOP_K])
        xs = take_rows(x1b, slot_tok)
        ys_moe = _expert_ffn(l, xs, block_exp, n_used, w_gu_b, b_gu4, w_dn_b, b_dn4, slot_gate.reshape(-1, 1))
        pos = pos.reshape(N, TOP_K)
        ysg = [take_rows(ys_moe, pos[:, kth]) for kth in range(TOP_K)]
        xf, xb2 = _combine(x1, ysg, row(ln2_g[l]), row(ln2_b[l]))
        xb = xb2.reshape(B, T, D)
    return xf.reshape(B, T, D)
```

```python
import functools
import math

import numpy as np
import jax
import jax.numpy as jnp
from jax import lax
from jax.experimental import pallas as pl
from jax.experimental.pallas import tpu as pltpu

F32 = jnp.float32
BF16 = jnp.bfloat16
HI = lax.Precision.HIGHEST

D_MODEL = 2048
GROUP = 512
N_EXPERTS = 32
TOP_K = 4
D_FF = 2048
DEPTH = 2
NORM_EPS = 1e-5
DN_ALPHA = (2 * DEPTH) ** 0.25
ROPE_BASE = 10000.0
GLA_TAU = 16.0
SWIGLU_ALPHA = 1.702
SWIGLU_LIMIT = 7.0

LANE = 128
TOKEN_BLOCK = 256
VMEM_LIMIT = 56 * 1024 * 1024

RET_OFF, RWKV_OFF, GLA_OFF, MLSTM_OFF, D_IN = 0, 1536, 3232, 4784, 6328
RET_W, RWKV_W, GLA_W, MLSTM_W = 1536, 1792, 1664, 1664


def _ret_cols():
    def half_split(base):
        lo = [base + h * 64 + j for h in range(4) for j in range(32)]
        hi = [base + h * 64 + 32 + j for h in range(4) for j in range(32)]
        return lo + hi
    cols = half_split(RET_OFF) + half_split(RET_OFF + 256)
    cols += list(range(RET_OFF + 512, RET_OFF + 1536))
    return np.asarray(cols, np.int32)


def _rwkv_cols():
    cols = list(range(RWKV_OFF, RWKV_OFF + 1696)) + [D_IN] * 96
    return np.asarray(cols, np.int32)


def _gla_cols():
    o = GLA_OFF
    cols = list(range(o, o + 1024)) + list(range(o + 1040, o + 1552)) + list(range(o + 1024, o + 1040)) + [D_IN] * 112
    return np.asarray(cols, np.int32)


def _mlstm_cols():
    o = MLSTM_OFF
    cols = list(range(o, o + 1024)) + list(range(o + 1032, o + 1544)) + list(range(o + 1024, o + 1032)) + [D_IN] * 120
    return np.asarray(cols, np.int32)


def _dot(a, b):
    return jnp.dot(a.astype(BF16), b.astype(BF16), preferred_element_type=F32)


def _dot_nt(a, b):
    return lax.dot_general(a.astype(BF16), b.astype(BF16), (((1,), (1,)), ((), ())), preferred_element_type=F32)


def _dot_tn(a, b):
    return lax.dot_general(a.astype(BF16), b.astype(BF16), (((0,), (0,)), ((), ())), preferred_element_type=F32)


def _split(a, terms):
    pieces, rem = [], a
    for t in range(terms):
        piece = rem.astype(BF16)
        pieces.append(piece)
        if t + 1 < terms:
            rem = rem - piece.astype(F32)
    return pieces


def _dot_sel(a, sel, terms):
    out = None
    for piece in _split(a, terms):
        d = jnp.dot(piece, sel, preferred_element_type=F32)
        out = d if out is None else out + d
    return out


def _sel_dot(sel, b, terms):
    out = None
    for piece in _split(b, terms):
        d = jnp.dot(sel, piece, preferred_element_type=F32)
        out = d if out is None else out + d
    return out


def _silu(x):
    return x * jax.nn.sigmoid(x)


def _head_norm_lanes(o, width):
    parts = []
    for h in range(o.shape[1] // width):
        oh = o[:, h * width:(h + 1) * width]
        mu = jnp.mean(oh, axis=-1, keepdims=True)
        d = oh - mu
        var = jnp.mean(d * d, axis=-1, keepdims=True)
        parts.append(d * lax.rsqrt(var + NORM_EPS))
    return jnp.concatenate(parts, axis=1)


RET_L = 128


def _ret_tables():
    lg = np.log(1.0 - 2.0 ** (-5.0 - np.arange(4, dtype=np.float64)))
    i = np.arange(RET_L, dtype=np.float64)
    lane = np.arange(256)
    head_of_lane = (lane % 128) // 32
    ms = np.zeros((4 * RET_L, 256), np.float32)
    mq = np.zeros((4 * RET_L, 256), np.float32)
    dk = np.zeros((4 * RET_L, 256), np.float32)
    di = np.zeros((4 * RET_L, RET_L), np.float32)
    for h in range(4):
        m = (head_of_lane == h).astype(np.float64)
        ms[h * RET_L:(h + 1) * RET_L] = m[None, :]
        mq[h * RET_L:(h + 1) * RET_L] = np.exp((i + 1.0) * lg[h])[:, None] * m[None, :]
        dk[h * RET_L:(h + 1) * RET_L] = np.exp((RET_L - 1.0 - i) * lg[h])[:, None]
        rel = i[:, None] - i[None, :]
        di[h * RET_L:(h + 1) * RET_L] = np.where(rel >= 0, np.exp(np.maximum(rel, 0.0) * lg[h]), 0.0)
    dchunk = np.exp(RET_L * lg).astype(np.float32)
    return ms, mq, dk, di, dchunk


def _ret_kernel(x_ref, w_ref, cos_ref, sin_ref, ms_ref, mq_ref, dk_ref, di_ref, g_ref, y_ref, s_ref, *, dchunk):
    @pl.when(pl.program_id(1) == 0)
    def _():
        s_ref[...] = jnp.zeros_like(s_ref)

    p = jnp.dot(x_ref[0], w_ref[...], preferred_element_type=F32)
    L = RET_L
    state = [s_ref[:, h * 128:(h + 1) * 128] for h in range(4)]
    for c in range(TOKEN_BLOCK // L):
        sl = slice(c * L, (c + 1) * L)
        q, k = p[sl, 0:256], p[sl, 256:512]
        v, gate = p[sl, 512:1024], p[sl, 1024:1536]
        cs, sn = cos_ref[sl, :], sin_ref[sl, :]

        def rope(t):
            t1, t2 = t[:, :128], t[:, 128:]
            return jnp.concatenate([t1 * cs - t2 * sn, t1 * sn + t2 * cs], axis=1)

        qr = rope(q) * (64.0 ** -0.5)
        kr = rope(k)
        q4 = jnp.concatenate([qr] * 4, axis=0)
        scores = _dot_nt(q4 * ms_ref[...], kr) * di_ref[...]
        qd = q4 * mq_ref[...]
        k4 = jnp.concatenate([kr] * 4, axis=0) * dk_ref[...]
        hbs = [slice(h * L, (h + 1) * L) for h in range(4)]
        vhs = [v[:, h * 128:(h + 1) * 128] for h in range(4)]
        inter = [_dot(qd[hbs[h]], state[h]) for h in range(4)]
        upd = [_dot_tn(k4[hbs[h]], vhs[h]) for h in range(4)]
        intra = [_dot(scores[hbs[h]], vhs[h]) for h in range(4)]
        outs = [intra[h] + inter[h] for h in range(4)]
        state = [dchunk[h] * state[h] + upd[h] for h in range(4)]
        o = _head_norm_lanes(jnp.concatenate(outs, axis=1), 128)
        y_ref[0, sl, :] = (o * g_ref[...] * _silu(gate)).astype(y_ref.dtype)
    for h in range(4):
        s_ref[:, h * 128:(h + 1) * 128] = state[h]


def _retention(xb, w, cos, sin, g, tables):
    B, T, _ = xb.shape
    ms, mq, dk, di, dchunk = tables
    const = lambda shape: pl.BlockSpec(shape, lambda b, t: (0,) * len(shape))
    return pl.pallas_call(
        functools.partial(_ret_kernel, dchunk=tuple(float(d) for d in dchunk)),
        grid=(B, T // TOKEN_BLOCK),
        in_specs=[
            pl.BlockSpec((1, TOKEN_BLOCK, D_MODEL), lambda b, t: (b, t, 0)),
            const((D_MODEL, RET_W)),
            pl.BlockSpec((TOKEN_BLOCK, 128), lambda b, t: (t, 0)),
            pl.BlockSpec((TOKEN_BLOCK, 128), lambda b, t: (t, 0)),
            const((4 * RET_L, 256)), const((4 * RET_L, 256)), const((4 * RET_L, 256)), const((4 * RET_L, RET_L)),
            const((1, GROUP)),
        ],
        out_specs=pl.BlockSpec((1, TOKEN_BLOCK, GROUP), lambda b, t: (b, t, 0)),
        out_shape=jax.ShapeDtypeStruct((B, T, GROUP), BF16),
        scratch_shapes=[pltpu.VMEM((256, GROUP), F32)],
        compiler_params=pltpu.CompilerParams(dimension_semantics=("arbitrary", "arbitrary"),
                                             vmem_limit_bytes=VMEM_LIMIT),
        name="retention",
    )(xb, w, cos, sin, jnp.asarray(ms), jnp.asarray(mq), jnp.asarray(dk), jnp.asarray(di), g)


RWKV_L = 64


RWKV_HG = 2
RWKV_GW = RWKV_HG * 64


def _rwkv_tables(n_batch):
    lane = np.arange(RWKV_GW)
    row = np.arange(RWKV_GW)
    ms4 = (lane[None, :] // 64 == row[:, None] // 64).astype(np.float32)
    same = (row[:, None] // 64 == row[None, :] // 64)
    tt, ii = row[:, None] % 64, row[None, :] % 64
    strict = (same & (tt > ii)).astype(np.float32)
    incl = (same & (tt >= ii)).astype(np.float32)
    rt = np.arange(n_batch * TOKEN_BLOCK)
    tri = ((rt[:, None] // 64 == rt[None, :] // 64) & (rt[:, None] >= rt[None, :])).astype(np.float32)
    r5 = np.arange(512)
    bd = (r5[:, None] // 64 == r5[None, :] // 64).astype(np.float32)
    return ms4, strict, incl, tri, bd


def _rwkv_kernel(x_ref, w_ref, mu_ref, w0_ref, w2_ref, a0_ref, a2_ref, g2_ref, kk_ref, ka_ref, rk_ref, gn_ref,
                 ms4_ref, strict_ref, incl_ref, tri_ref, bd_ref, y_ref, p_scr, s_scr):
    NB, TC, L = x_ref.shape[0], TOKEN_BLOCK, RWKV_L
    GW, NG = RWKV_GW, GROUP // RWKV_GW

    @pl.when(pl.program_id(0) == 0)
    def _():
        s_scr[...] = jnp.zeros_like(s_scr)
        for b in range(NB):
            p_scr[b, pl.ds(0, 8), :] = jnp.zeros((8, RWKV_W), F32)

    p = jnp.dot(x_ref[...].reshape(NB * TC, D_MODEL), w_ref[...], preferred_element_type=F32)
    prevs = []
    for b in range(NB):
        pb = p[b * TC:(b + 1) * TC]
        p_scr[b, pl.ds(8, TC), :] = pb
        prevs.append(p_scr[b, pl.ds(7, TC), :])
        p_scr[b, pl.ds(0, 8), :] = pb[TC - 8:, :]
    prev = jnp.concatenate(prevs, axis=0)
    cols = p + (prev - p) * mu_ref[...]
    r, k, v = cols[:, 0:512], cols[:, 512:1024], cols[:, 1024:1536]
    tail = cols[:, 1536:1792]
    w = -jax.nn.softplus(-(w0_ref[...] + _dot(jnp.tanh(tail), w2_ref[...]))) - 0.5
    logdec = -jnp.exp(w)
    a = jax.nn.sigmoid(a0_ref[...] + _dot(tail, a2_ref[...]))
    g = _dot(jax.nn.sigmoid(tail), g2_ref[...])
    kk = k * kk_ref[...]
    bd = bd_ref[...]
    kk = kk / jnp.maximum(jnp.sqrt(_dot_sel(kk * kk, bd, 1)), 1e-12)
    k = k * (1.0 + (a - 1.0) * ka_ref[...])
    bonus = _dot_sel(r * k * rk_ref[...], bd, 1) * v
    cum = _sel_dot(tri_ref[...], logdec, 2)
    beta = kk * a
    ms4 = ms4_ref[...]
    strict, incl = strict_ref[...], incl_ref[...]

    def stack(t):
        return jnp.concatenate([t] * RWKV_HG, axis=0) * ms4

    chains = [(b, gi) for b in range(NB) for gi in range(NG)]
    CH = range(len(chains))
    st = [s_scr[b, gi] for b, gi in chains]
    outs = [[[None] * NG for _ in range(TC // L)] for _ in range(NB)]
    for c in range(TC // L):
        x1, x2, yy, z1, z2, sv, cl = ([None] * len(chains) for _ in range(7))
        for i, (b, gi) in enumerate(chains):
            rows = slice(b * TC + c * L, b * TC + (c + 1) * L)
            lanes = slice(gi * GW, (gi + 1) * GW)
            cm, lw = cum[rows, lanes], logdec[rows, lanes]
            cl[i] = cm[L - 1:L, :]
            gam, gam_ex, igam, gam_r = jnp.exp(cm), jnp.exp(cm - lw), jnp.exp(-cm), jnp.exp(cl[i] - cm)
            kk_c, r_c, k_c, v_c, b_c = kk[rows, lanes], r[rows, lanes], k[rows, lanes], v[rows, lanes], beta[rows, lanes]
            x1[i], x2[i] = stack(-kk_c * gam_ex), stack(r_c * gam)
            yy[i] = jnp.concatenate([stack(b_c * igam), stack(k_c * igam)], axis=0)
            z1[i], z2[i] = stack(b_c * gam_r), stack(k_c * gam_r)
            sv[i] = stack(v_c)
        ab = [_dot_nt(jnp.concatenate([x1[i], x2[i]], axis=0), yy[i]) for i in CH]
        a_b = [ab[i][0:GW, 0:GW] * strict for i in CH]
        a_k = [ab[i][0:GW, GW:2 * GW] * strict for i in CH]
        b_b = [ab[i][GW:2 * GW, 0:GW] * incl for i in CH]
        b_k = [ab[i][GW:2 * GW, GW:2 * GW] * incl for i in CH]
        u = [_dot(a_k[i], sv[i]) for i in CH]
        t0 = [_dot_nt(x1[i], st[i]) for i in CH]
        u = [u[i] + t0[i] for i in CH]
        pw = a_b
        for it in range(6):
            t = [_dot(pw[i], u[i]) for i in CH]
            if it < 5:
                pw = [_dot(pw[i], pw[i]) for i in CH]
            u = [u[i] + t[i] for i in CH]
        y0 = [_dot_nt(x2[i], st[i]) for i in CH]
        y1 = [_dot(b_b[i], u[i]) for i in CH]
        y2 = [_dot(b_k[i], sv[i]) for i in CH]
        s1 = [_dot_tn(u[i], z1[i]) for i in CH]
        s2 = [_dot_tn(sv[i], z2[i]) for i in CH]
        for i, (b, gi) in enumerate(chains):
            yst = y0[i] + y1[i] + y2[i]
            outs[b][c][gi] = sum(yst[h * L:(h + 1) * L] for h in range(RWKV_HG))
            st[i] = st[i] * jnp.exp(cl[i]) + s1[i] + s2[i]
    for i, (b, gi) in enumerate(chains):
        s_scr[b, gi] = st[i]

    y = jnp.concatenate([jnp.concatenate(o, axis=1) for ob in outs for o in ob], axis=0)
    mean = _dot_sel(y, bd, 1) * (1.0 / 64.0)
    d = y - mean
    var = _dot_sel(d * d, bd, 1) * (1.0 / 64.0)
    yn = d * lax.rsqrt(var + NORM_EPS) * gn_ref[...]
    y_ref[...] = ((yn + bonus) * g).astype(y_ref.dtype).reshape(NB, TC, GROUP)


def _rwkv(xb, w, mu, w0, w2p, a0, a2p, g2p, k_k, k_a, r_k, gn, tables):
    B, T, _ = xb.shape
    const = lambda shape: pl.BlockSpec(shape, lambda t: (0,) * len(shape))
    row = lambda n: const((1, n))
    return pl.pallas_call(
        _rwkv_kernel,
        grid=(T // TOKEN_BLOCK,),
        in_specs=[
            pl.BlockSpec((B, TOKEN_BLOCK, D_MODEL), lambda t: (0, t, 0)),
            const((D_MODEL, RWKV_W)), row(RWKV_W), row(GROUP), const((256, GROUP)), row(GROUP), const((256, GROUP)),
            const((256, GROUP)), row(GROUP), row(GROUP), row(GROUP), row(GROUP),
            const((RWKV_GW, RWKV_GW)), const((RWKV_GW, RWKV_GW)), const((RWKV_GW, RWKV_GW)),
            const((B * TOKEN_BLOCK, B * TOKEN_BLOCK)), const((GROUP, GROUP)),
        ],
        out_specs=pl.BlockSpec((B, TOKEN_BLOCK, GROUP), lambda t: (0, t, 0)),
        out_shape=jax.ShapeDtypeStruct((B, T, GROUP), BF16),
        scratch_shapes=[pltpu.VMEM((B, TOKEN_BLOCK + 8, RWKV_W), F32),
                        pltpu.VMEM((B, GROUP // RWKV_GW, RWKV_GW, RWKV_GW), F32)],
        compiler_params=pltpu.CompilerParams(dimension_semantics=("arbitrary",), vmem_limit_bytes=VMEM_LIMIT),
        name="rwkv7",
    )(xb, w, mu, w0, w2p, a0, a2p, g2p, k_k, k_a, r_k, gn,
      *[jnp.asarray(t, dt) for t, dt in zip(tables(B), (F32, F32, F32, BF16, BF16))])


GLA_SUB = 16


def _gla_tables():
    r = np.arange(256)
    tri16 = ((r[:, None] // 16 == r[None, :] // 16) & (r[:, None] >= r[None, :])).astype(np.float32)
    sel = (r[:, None] // 64 == np.arange(512)[None, :] // 128).astype(np.float32)
    ms = (np.arange(256)[None, :] // 64 == np.arange(64)[:, None] // 16).astype(np.float32)
    return tri16, sel, ms


def _gla_kernel(x_ref, w_ref, g2_ref, gb_ref, gn_ref, tri_ref, sel_ref, ms_ref, y_ref, st_scr):
    TC, SUB = TOKEN_BLOCK, GLA_SUB

    @pl.when(pl.program_id(1) == 0)
    def _():
        st_scr[...] = jnp.zeros_like(st_scr)

    p = jnp.dot(x_ref[0], w_ref[...], preferred_element_type=F32)
    gate = p[:, 1024:1536]
    la = jax.nn.log_sigmoid(_dot(p[:, 1536:1664], g2_ref[...]) + gb_ref[...]) * (1.0 / GLA_TAU)
    q, k, v = p[:, 0:256] * (64.0 ** -0.5), p[:, 256:512], p[:, 512:1024]
    b = _sel_dot(tri_ref[...], la, 3)
    sel = sel_ref[...]
    ms = ms_ref[...]
    row = lax.broadcasted_iota(jnp.int32, (SUB, 1), 0)
    outs = []
    st = st_scr[...]
    for s in range(TC // SUB):
        rows = slice(s * SUB, (s + 1) * SUB)
        qb, kb, vb, bb = q[rows], k[rows], v[rows], b[rows]
        bl = bb[SUB - 1:SUB, :]
        qe = jnp.concatenate([qb * jnp.exp(bb)] * 4, axis=0) * ms
        o_inter = jnp.concatenate(
            [_dot_nt(qe[h * SUB:(h + 1) * SUB], st[h * 128:(h + 1) * 128, :]) for h in range(4)], axis=1)
        ts = []
        for j in range(SUB):
            diff = jnp.where(row >= j, bb - bb[j:j + 1, :], -1e30)
            ts.append(qb * kb[j:j + 1, :] * jnp.exp(diff))
        rr = jnp.dot(jnp.concatenate(ts, axis=0).astype(BF16), sel, preferred_element_type=F32)
        o_intra = rr[0:SUB] * vb[0:1, :]
        for j in range(1, SUB):
            o_intra = o_intra + rr[j * SUB:(j + 1) * SUB] * vb[j:j + 1, :]
        outs.append(o_intra + o_inter)
        st = st * jnp.exp(bl) + _dot_tn(vb, kb * jnp.exp(bl - bb))
    st_scr[...] = st
    o = _head_norm_lanes(jnp.concatenate(outs, axis=0), 128)
    y_ref[0] = (o * gn_ref[...] * _silu(gate)).astype(y_ref.dtype)


def _gla(xb, w, g2p, gb, gn, tables):
    B, T, _ = xb.shape
    const = lambda shape: pl.BlockSpec(shape, lambda b, t: (0,) * len(shape))
    return pl.pallas_call(
        _gla_kernel,
        grid=(B, T // TOKEN_BLOCK),
        in_specs=[
            pl.BlockSpec((1, TOKEN_BLOCK, D_MODEL), lambda b, t: (b, t, 0)),
            const((D_MODEL, GLA_W)), const((128, 256)), const((1, 256)), const((1, GROUP)),
            const((256, 256)), const((256, 512)), const((64, 256)),
        ],
        out_specs=pl.BlockSpec((1, TOKEN_BLOCK, GROUP), lambda b, t: (b, t, 0)),
        out_shape=jax.ShapeDtypeStruct((B, T, GROUP), BF16),
        scratch_shapes=[pltpu.VMEM((GROUP, 256), F32)],
        compiler_params=pltpu.CompilerParams(dimension_semantics=("arbitrary", "arbitrary"),
                                             vmem_limit_bytes=VMEM_LIMIT),
        name="gla",
    )(xb, w, g2p, gb, gn, *[jnp.asarray(t, dt) for t, dt in zip(tables, (BF16, BF16, F32))])


MLSTM_L = 128


def _mlstm_tables():
    r = np.arange(TOKEN_BLOCK)
    tri = ((r[:, None] // MLSTM_L == r[None, :] // MLSTM_L) & (r[:, None] >= r[None, :])).astype(np.float32)
    ms = (np.arange(256)[None, :] // 64 == np.arange(4 * MLSTM_L)[:, None] // MLSTM_L).astype(np.float32)
    return tri, ms


def _mlstm_kernel(x_ref, w_ref, conv_ref, gbias_ref, gn_ref, tri_ref, ms_ref, y_ref,
                  qk_scr, c_scr, n_scr, m_scr):
    TC, L = TOKEN_BLOCK, MLSTM_L

    @pl.when(pl.program_id(1) == 0)
    def _():
        c_scr[...] = jnp.zeros_like(c_scr)
        n_scr[...] = jnp.zeros_like(n_scr)
        m_scr[...] = jnp.zeros_like(m_scr)
        qk_scr[pl.ds(0, 8), :] = jnp.zeros((8, 512), F32)

    p = jnp.dot(x_ref[0], w_ref[...], preferred_element_type=F32)
    qk_scr[pl.ds(8, TC), :] = p[:, 0:512]
    conv = p[:, 0:512] * conv_ref[3:4, :]
    for j in range(3):
        conv = conv + qk_scr[pl.ds(5 + j, TC), :] * conv_ref[j:j + 1, :]
    qk_scr[pl.ds(0, 8), :] = p[TC - 8:, 0:512]
    qk = _silu(conv)
    q, k = qk[:, 0:256], qk[:, 256:512] * (64.0 ** -0.5)
    v, og = p[:, 512:1024], p[:, 1024:1536]
    tail = p[:, 1536:1664] + gbias_ref[...]
    lane = lax.broadcasted_iota(jnp.int32, (1, 128), 1)
    logf = jnp.where((lane >= 4) & (lane < 8), jax.nn.log_sigmoid(tail), 0.0)
    cumf = _sel_dot(tri_ref[...], logf, 3)
    gcol = jnp.where(lane < 4, tail, cumf)
    eye = (lax.broadcasted_iota(jnp.int32, (128, 128), 0) == lax.broadcasted_iota(jnp.int32, (128, 128), 1)).astype(BF16)
    ms = ms_ref[...]
    ri = lax.broadcasted_iota(jnp.int32, (L, L), 0)
    ci = lax.broadcasted_iota(jnp.int32, (L, L), 1)
    causal = ri >= ci
    lane256 = lax.broadcasted_iota(jnp.int32, (1, 256), 1)
    c_state = [c_scr[:, h * 128:(h + 1) * 128] for h in range(4)]
    m_state = [m_scr[0:1, h:h + 1] for h in range(4)]
    n_state = n_scr[...]

    for c in range(TC // L):
        rows = slice(c * L, (c + 1) * L)
        gc = gcol[rows, :]
        gr = sum(lax.dot_general(eye, piece, (((1,), (1,)), ((), ())), preferred_element_type=F32)
                 for piece in _split(gc, 3))
        qc, kc, vc = q[rows], k[rows], v[rows]
        q4 = jnp.concatenate([qc] * 4, axis=0) * ms
        scores = _dot_nt(q4, kc)
        qn = jnp.sum(q4 * n_state, axis=1, keepdims=True)
        hbs = [slice(h * L, (h + 1) * L) for h in range(4)]
        vhs = [vc[:, h * 128:(h + 1) * 128] for h in range(4)]
        qc_state = [_dot(q4[hbs[h]], c_state[h]) for h in range(4)]
        ss, inters, m_rows, wj_cols, wrows, scales = [], [], [], [], [], []
        n_scale = jnp.zeros((1, 256), F32)
        for h in range(4):
            b_col, i_col = gc[:, 4 + h:5 + h], gc[:, h:h + 1]
            b_row, i_row = gr[4 + h:5 + h, :], gr[h:h + 1, :]
            m_prev = m_state[h]
            log_d = jnp.where(causal, b_col - b_row + i_row, -jnp.inf)
            m_inter = b_col + m_prev
            m_row = jnp.maximum(jnp.max(log_d, axis=-1, keepdims=True), m_inter)
            ss.append(scores[hbs[h]] * jnp.exp(log_d - m_row))
            inters.append(jnp.exp(m_inter - m_row))
            m_rows.append(m_row)
            m_new = m_row[L - 1:L, :]
            b_last = b_col[L - 1:L, :]
            wj_cols.append(jnp.exp(b_last - b_col + i_col - m_new))
            wrows.append(jnp.exp(b_last - b_row + i_row - m_new))
            scale = jnp.exp(b_last + m_prev - m_new)
            scales.append(scale)
            n_scale = jnp.where(lane256 // 64 == h, scale, n_scale)
            m_state[h] = m_new
        sv = [_dot(ss[h], vhs[h]) for h in range(4)]
        upd = [_dot_tn(kc * wj_cols[h], vhs[h]) for h in range(4)]
        outs = []
        for h in range(4):
            num = sv[h] + inters[h] * qc_state[h]
            den = jnp.sum(ss[h], axis=-1, keepdims=True) + inters[h] * qn[hbs[h]]
            outs.append(num / jnp.maximum(jnp.abs(den), jnp.exp(-m_rows[h])))
            c_state[h] = scales[h] * c_state[h] + upd[h]
        wk = _dot(jnp.concatenate(wrows + [jnp.zeros((4, L), F32)], axis=0), kc)
        n_add = jnp.zeros((1, 256), F32)
        for h in range(4):
            n_add = jnp.where(lane256 // 64 == h, wk[h:h + 1, :], n_add)
        n_state = n_scale * n_state + n_add
        o = _head_norm_lanes(jnp.concatenate(outs, axis=1), 128)
        y_ref[0, rows, :] = (o * gn_ref[...] * jax.nn.sigmoid(og[rows])).astype(y_ref.dtype)

    n_scr[...] = n_state
    for h in range(4):
        c_scr[:, h * 128:(h + 1) * 128] = c_state[h]
        m_scr[0:1, h:h + 1] = m_state[h]


def _mlstm(xb, w, conv_w, gbias, gn, tables):
    B, T, _ = xb.shape
    const = lambda shape: pl.BlockSpec(shape, lambda b, t: (0,) * len(shape))
    return pl.pallas_call(
        _mlstm_kernel,
        grid=(B, T // TOKEN_BLOCK),
        in_specs=[
            pl.BlockSpec((1, TOKEN_BLOCK, D_MODEL), lambda b, t: (b, t, 0)),
            const((D_MODEL, MLSTM_W)), const((4, 512)), const((1, 128)), const((1, GROUP)),
            const((TOKEN_BLOCK, TOKEN_BLOCK)), const((4 * MLSTM_L, 256)),
        ],
        out_specs=pl.BlockSpec((1, TOKEN_BLOCK, GROUP), lambda b, t: (b, t, 0)),
        out_shape=jax.ShapeDtypeStruct((B, T, GROUP), BF16),
        scratch_shapes=[pltpu.VMEM((TOKEN_BLOCK + 8, 512), F32), pltpu.VMEM((256, GROUP), F32),
                        pltpu.VMEM((1, 256), F32), pltpu.VMEM((8, 128), F32)],
        compiler_params=pltpu.CompilerParams(dimension_semantics=("arbitrary", "arbitrary"),
                                             vmem_limit_bytes=VMEM_LIMIT),
        name="mlstm",
    )(xb, w, conv_w, gbias, gn, *[jnp.asarray(t, dt) for t, dt in zip(tables, (BF16, F32))])


ROW_BLOCK = 512


def _layer_norm(z, g, b):
    mu = jnp.mean(z, axis=-1, keepdims=True)
    d = z - mu
    var = jnp.mean(d * d, axis=-1, keepdims=True)
    return d * lax.rsqrt(var + NORM_EPS) * g + b


def _outproj_kernel(y0_ref, y1_ref, y2_ref, y3_ref, x_ref, w_ref, g_ref, b_ref, rwh_ref, rwl_ref, rb_ref,
                    x1_ref, x1b_ref, idx_ref, gate_ref):
    acc = jnp.dot(y0_ref[...], w_ref[0:512, :], preferred_element_type=F32)
    acc += jnp.dot(y1_ref[...], w_ref[512:1024, :], preferred_element_type=F32)
    acc += jnp.dot(y2_ref[...], w_ref[1024:1536, :], preferred_element_type=F32)
    acc += jnp.dot(y3_ref[...], w_ref[1536:2048, :], preferred_element_type=F32)
    x1 = _layer_norm(DN_ALPHA * x_ref[...] + acc, g_ref[...], b_ref[...])
    x1_ref[...] = x1
    x1b_ref[...] = x1.astype(BF16)
    x_hi, x_lo = _split(x1, 2)
    logits = (jnp.dot(x_hi, rwh_ref[...], preferred_element_type=F32)
              + jnp.dot(x_lo, rwh_ref[...], preferred_element_type=F32)
              + jnp.dot(x_hi, rwl_ref[...], preferred_element_type=F32)) + rb_ref[...]
    lane = lax.broadcasted_iota(jnp.int32, logits.shape, 1)
    logits = jnp.where(lane < N_EXPERTS, logits, -jnp.inf)
    idx_out = jnp.zeros(logits.shape, jnp.int32)
    gate_out = jnp.zeros(logits.shape, F32)
    top = None
    for kth in range(TOP_K):
        m = jnp.max(logits, axis=-1, keepdims=True)
        sel = jnp.min(jnp.where(logits == m, lane, LANE), axis=-1, keepdims=True)
        if kth == 0:
            top = m
        idx_out = jnp.where(lane == kth, sel, idx_out)
        gate_out = jnp.where(lane == kth, jnp.exp(m - top), gate_out)
        logits = jnp.where(lane == sel, -jnp.inf, logits)
    gate_out = gate_out / jnp.sum(gate_out, axis=-1, keepdims=True)
    idx_ref[...] = idx_out
    gate_ref[...] = gate_out


def _outproj(ys, x, w_out, g, b, rw, rb):
    N = x.shape[0]
    rb_ = ROW_BLOCK
    const = lambda shape: pl.BlockSpec(shape, lambda i: (0,) * len(shape))
    rows = lambda width: pl.BlockSpec((rb_, width), lambda i: (i, 0))
    rw_hi = rw.astype(BF16)
    rw_lo = (rw - rw_hi.astype(F32)).astype(BF16)
    return pl.pallas_call(
        _outproj_kernel,
        grid=(N // rb_,),
        in_specs=[rows(GROUP)] * 4 + [rows(D_MODEL), const((D_MODEL, D_MODEL)), const((1, D_MODEL)), const((1, D_MODEL)),
                                      const((D_MODEL, LANE)), const((D_MODEL, LANE)), const((1, LANE))],
        out_specs=[rows(D_MODEL), rows(D_MODEL), rows(LANE), rows(LANE)],
        out_shape=[jax.ShapeDtypeStruct((N, D_MODEL), F32), jax.ShapeDtypeStruct((N, D_MODEL), BF16),
                   jax.ShapeDtypeStruct((N, LANE), jnp.int32), jax.ShapeDtypeStruct((N, LANE), F32)],
        compiler_params=pltpu.CompilerParams(dimension_semantics=("arbitrary",), vmem_limit_bytes=VMEM_LIMIT),
        name="outproj_norm_router",
    )(*ys, x, w_out, g, b, rw_hi, rw_lo, rb)


MOE_BLOCK = 512
FF_TILE = 512


def _ffn_schedule(pcounts, pstarts, n_blocks, n_tiles):
    nblk = (pcounts // MOE_BLOCK).astype(jnp.int32)
    bstart = (pstarts // MOE_BLOCK).astype(jnp.int32)
    steps_e = nblk * n_tiles
    ends = jnp.cumsum(steps_e).astype(jnp.int32)
    begins = ends - steps_e
    total = ends[-1]
    s = jnp.arange(n_blocks * n_tiles, dtype=jnp.int32)
    e = jnp.minimum(jnp.searchsorted(ends, s, side='right'), N_EXPERTS - 1).astype(jnp.int32)
    r = s - begins[e]
    nb_e = jnp.maximum(nblk[e], 1)
    til, j = r // nb_e, r % nb_e
    blk = bstart[e] + j
    last = jnp.maximum(total - 1, 0)
    valid = s < total
    rp = s - total
    blk = jnp.where(valid, blk, total // n_tiles + rp // n_tiles)
    otil = jnp.where(valid, til, rp % n_tiles)
    wtil = jnp.where(valid, til, til[last])
    e = jnp.where(valid, e, e[last])
    first = (valid & (j == 0)).astype(jnp.int32)
    i32 = lambda a: a.astype(jnp.int32)
    return i32(blk), i32(otil), i32(wtil), i32(e), first, i32(total).reshape(1)


def _ffn_up_kernel(blk_ref, otil_ref, wtil_ref, ex_ref, first_ref, ns_ref,
                   xs_ref, wg_ref, wl_ref, bg_ref, bl_ref, act_ref, wg_bf, wl_bf):
    s = pl.program_id(0)

    @pl.when(first_ref[s] == 1)
    def _():
        wg_bf[...] = wg_ref[...].astype(BF16)
        wl_bf[...] = wl_ref[...].astype(BF16)

    @pl.when(s < ns_ref[0])
    def _():
        xs = xs_ref[...]
        glu = jnp.dot(xs, wg_bf[...], preferred_element_type=F32) + bg_ref[...]
        lin = jnp.dot(xs, wl_bf[...], preferred_element_type=F32) + bl_ref[...]
        glu = jnp.minimum(glu, SWIGLU_LIMIT)
        lin = jnp.clip(lin, -SWIGLU_LIMIT, SWIGLU_LIMIT)
        act_ref[...] = (glu * jax.nn.sigmoid(SWIGLU_ALPHA * glu) * (lin + 1.0)).astype(act_ref.dtype)

    @pl.when(s >= ns_ref[0])
    def _():
        act_ref[...] = jnp.zeros_like(act_ref)


def _ffn_dn_kernel(blk_ref, otil_ref, wtil_ref, ex_ref, first_ref, ns_ref,
                   act_ref, wd_ref, bd_ref, gate_ref, o_ref, wd_bf):
    s = pl.program_id(0)

    @pl.when(first_ref[s] == 1)
    def _():
        wd_bf[...] = wd_ref[...].astype(BF16)

    @pl.when(s < ns_ref[0])
    def _():
        out = jnp.dot(act_ref[...], wd_bf[...], preferred_element_type=F32) + bd_ref[...]
        o_ref[...] = (out * gate_ref[...]).astype(o_ref.dtype)

    @pl.when(s >= ns_ref[0])
    def _():
        o_ref[...] = jnp.zeros_like(o_ref)


def _expert_ffn(layer, xs, sched, w_gu, b_gu, w_dn, b_dn, slot_gate):
    n_slots = xs.shape[0]
    nt = D_FF // FF_TILE
    n_steps = (n_slots // MOE_BLOCK) * nt
    params = pltpu.CompilerParams(dimension_semantics=("arbitrary",), vmem_limit_bytes=VMEM_LIMIT)
    wtile = lambda off: (lambda s, blk, otil, wtil, ex, first, ns: (layer, ex[s], 0, off + wtil[s]))
    rows_full = lambda s, blk, otil, wtil, ex, first, ns: (blk[s], 0)
    rows_tile = lambda s, blk, otil, wtil, ex, first, ns: (blk[s], otil[s])
    act = pl.pallas_call(
        _ffn_up_kernel,
        grid_spec=pltpu.PrefetchScalarGridSpec(
            num_scalar_prefetch=6,
            grid=(n_steps,),
            in_specs=[
                pl.BlockSpec((MOE_BLOCK, D_MODEL), rows_full),
                pl.BlockSpec((None, None, D_MODEL, FF_TILE), wtile(0)),
                pl.BlockSpec((None, None, D_MODEL, FF_TILE), wtile(nt)),
                pl.BlockSpec((None, None, 1, FF_TILE), wtile(0)),
                pl.BlockSpec((None, None, 1, FF_TILE), wtile(nt)),
            ],
            out_specs=pl.BlockSpec((MOE_BLOCK, FF_TILE), rows_tile),
            scratch_shapes=[pltpu.VMEM((D_MODEL, FF_TILE), BF16), pltpu.VMEM((D_MODEL, FF_TILE), BF16)],
        ),
        out_shape=jax.ShapeDtypeStruct((n_slots, D_FF), BF16),
        compiler_params=params,
        name="expert_up",
    )(*sched, xs, w_gu, w_gu, b_gu, b_gu)
    return pl.pallas_call(
        _ffn_dn_kernel,
        grid_spec=pltpu.PrefetchScalarGridSpec(
            num_scalar_prefetch=6,
            grid=(n_steps,),
            in_specs=[
                pl.BlockSpec((MOE_BLOCK, D_FF), rows_full),
                pl.BlockSpec((None, None, D_FF, FF_TILE), wtile(0)),
                pl.BlockSpec((None, None, 1, FF_TILE), wtile(0)),
                pl.BlockSpec((MOE_BLOCK, 1), rows_full),
            ],
            out_specs=pl.BlockSpec((MOE_BLOCK, FF_TILE), rows_tile),
            scratch_shapes=[pltpu.VMEM((D_FF, FF_TILE), BF16)],
        ),
        out_shape=jax.ShapeDtypeStruct((n_slots, D_MODEL), BF16),
        compiler_params=params,
        name="expert_down",
    )(*sched, act, w_dn, b_dn, slot_gate)


def _combine_kernel(x_ref, y0_ref, y1_ref, y2_ref, y3_ref, g_ref, b_ref, o_ref, ob_ref):
    y = (y0_ref[...].astype(F32) + y1_ref[...].astype(F32)) + (y2_ref[...].astype(F32) + y3_ref[...].astype(F32))
    x2 = _layer_norm(DN_ALPHA * x_ref[...] + y, g_ref[...], b_ref[...])
    o_ref[...] = x2
    ob_ref[...] = x2.astype(BF16)


def _combine(x1, ysg, g, b):
    N = x1.shape[0]
    rb_ = ROW_BLOCK
    const = lambda shape: pl.BlockSpec(shape, lambda i: (0,) * len(shape))
    rows = pl.BlockSpec((rb_, D_MODEL), lambda i: (i, 0))
    return pl.pallas_call(
        _combine_kernel,
        grid=(N // rb_,),
        in_specs=[rows] * (1 + TOP_K) + [const((1, D_MODEL)), const((1, D_MODEL))],
        out_specs=[rows] * 2,
        out_shape=[jax.ShapeDtypeStruct((N, D_MODEL), F32), jax.ShapeDtypeStruct((N, D_MODEL), BF16)],
        compiler_params=pltpu.CompilerParams(dimension_semantics=("arbitrary",), vmem_limit_bytes=VMEM_LIMIT),
        name="combine_norm",
    )(x1, *ysg, g, b)


def _route(top_idx, gate):
    n_tok = top_idx.shape[0]
    n_assign = n_tok * TOP_K
    flat_e = top_idx.reshape(-1).astype(jnp.int32)
    flat_gate = gate.reshape(-1)
    iota = jnp.arange(n_assign, dtype=jnp.int32)
    se, order = lax.sort((flat_e, iota), num_keys=1)
    bounds = jnp.searchsorted(se, jnp.arange(N_EXPERTS + 1, dtype=jnp.int32), side='left').astype(jnp.int32)
    starts, counts = bounds[:-1], bounds[1:] - bounds[:-1]
    pcounts = (counts + MOE_BLOCK - 1) // MOE_BLOCK * MOE_BLOCK
    pends = jnp.cumsum(pcounts)
    pstarts = pends - pcounts
    dest = (pstarts[se] + iota - starts[se]).astype(jnp.int32)
    n_slots = -(-n_assign // MOE_BLOCK) * MOE_BLOCK + N_EXPERTS * MOE_BLOCK
    n_blocks = n_slots // MOE_BLOCK
    block_start = jnp.arange(n_blocks, dtype=jnp.int32) * MOE_BLOCK
    block_exp = jnp.minimum(jnp.searchsorted(pends, block_start, side='right'), N_EXPERTS - 1).astype(jnp.int32)
    slot = jnp.arange(n_slots, dtype=jnp.int32)
    slot_e = jnp.repeat(block_exp, MOE_BLOCK)
    off = slot - pstarts[slot_e]
    valid = off < counts[slot_e]
    src = order[jnp.clip(starts[slot_e] + off, 0, n_assign - 1)]
    slot_tok = jnp.where(valid, src // TOP_K, 0).astype(jnp.int32)
    slot_gate = jnp.where(valid, flat_gate[src], 0.0)
    _, pos = lax.sort((order, dest), num_keys=1)
    sched = _ffn_schedule(pcounts, pstarts, n_blocks, D_FF // FF_TILE)
    return slot_tok, slot_gate, pos, sched


def _pad_rows(w, row0, total):
    return jnp.zeros((total, w.shape[1]), F32).at[row0:row0 + w.shape[0]].set(w)


def kernel(x, w_in, rwkv_mu, rwkv_w0, rwkv_w2, rwkv_a0, rwkv_a2, rwkv_g2, rwkv_k_k, rwkv_k_a, rwkv_r_k, gla_g2, gla_gb, mlstm_conv, mlstm_ib, mlstm_fb, head_norm_g, w_out, ln1_g, ln1_b, router_w, router_b, w_gu, b_gu, w_dn, b_dn, ln2_g, ln2_b):
    B, T, D = x.shape
    N = B * T
    depth = w_in.shape[0]
    ret_tab, rwkv_tab, gla_tab, mlstm_tab = _ret_tables(), _rwkv_tables, _gla_tables(), _mlstm_tables()
    pos_t = jnp.arange(T, dtype=F32)
    inv = ROPE_BASE ** (-jnp.arange(32, dtype=F32) / 32)
    ang = pos_t[:, None] * inv[None, :]
    cos = jnp.tile(jnp.cos(ang), (1, 4))
    sin = jnp.tile(jnp.sin(ang), (1, 4))
    row = lambda v: v.reshape(1, -1)

    b_gu4 = b_gu.reshape(depth, N_EXPERTS, 1, 2 * D_FF)
    b_dn4 = b_dn.reshape(depth, N_EXPERTS, 1, D_MODEL)
    take_rows = lambda a, i: a.at[i].get(mode='promise_in_bounds')

    xf = x.reshape(N, D)
    xb = x.astype(BF16)
    for l in range(depth):
        w_ext = jnp.concatenate([w_in[l], jnp.zeros((D, 1), F32)], axis=1)
        seg = lambda cols: jnp.take(w_ext, jnp.asarray(cols), axis=1).astype(BF16)
        gn = head_norm_g[l]
        y_ret = _retention(xb, seg(_ret_cols()), cos, sin, row(gn[0:512]), ret_tab)
        mu = jnp.concatenate([rwkv_mu[l], jnp.zeros((96,), F32)])
        y_rwkv = _rwkv(xb, seg(_rwkv_cols()), row(mu), row(rwkv_w0[l]), _pad_rows(rwkv_w2[l], 0, 256),
                       row(rwkv_a0[l]), _pad_rows(rwkv_a2[l], 32, 256), _pad_rows(rwkv_g2[l], 64, 256),
                       row(rwkv_k_k[l]), row(rwkv_k_a[l]), row(rwkv_r_k[l]), row(gn[512:1024]), rwkv_tab)
        y_gla = _gla(xb, seg(_gla_cols()), _pad_rows(gla_g2[l], 0, 128), row(gla_gb[l]), row(gn[1024:1536]), gla_tab)
        gbias = jnp.concatenate([mlstm_ib[l], mlstm_fb[l], jnp.zeros((120,), F32)])
        y_mlstm = _mlstm(xb, seg(_mlstm_cols()), mlstm_conv[l], row(gbias), row(gn[1536:2048]), mlstm_tab)
        ys = [y.reshape(N, GROUP) for y in (y_ret, y_rwkv, y_gla, y_mlstm)]
        rw = jnp.zeros((D, LANE), F32).at[:, :N_EXPERTS].set(router_w[l])
        rb = jnp.zeros((1, LANE), F32).at[0, :N_EXPERTS].set(router_b[l])
        x1, x1b, idx, gate = _outproj(ys, xf, w_out[l].astype(BF16), row(ln1_g[l]), row(ln1_b[l]), rw, rb)
        slot_tok, slot_gate, pos, sched = _route(idx[:, :TOP_K], gate[:, :TOP_K])
        xs = take_rows(x1b, slot_tok)
        ys_moe = _expert_ffn(l, xs, sched, w_gu, b_gu4, w_dn, b_dn4, slot_gate.reshape(-1, 1))
        pos = pos.reshape(N, TOP_K)
        ysg = [take_rows(ys_moe, pos[:, kth]) for kth in range(TOP_K)]
        xf, xb2 = _combine(x1, ysg, row(ln2_g[l]), row(ln2_b[l]))
        xb = xb2.reshape(B, T, D)
    return xf.reshape(B, T, D)
```

```python
import functools
import math

import numpy as np
import jax
import jax.numpy as jnp
from jax import lax
from jax.experimental import pallas as pl
from jax.experimental.pallas import tpu as pltpu

F32 = jnp.float32
BF16 = jnp.bfloat16
HI = lax.Precision.HIGHEST

D_MODEL = 2048
GROUP = 512
N_EXPERTS = 32
TOP_K = 4
D_FF = 2048
DEPTH = 2
NORM_EPS = 1e-5
DN_ALPHA = (2 * DEPTH) ** 0.25
ROPE_BASE = 10000.0
GLA_TAU = 16.0
SWIGLU_ALPHA = 1.702
SWIGLU_LIMIT = 7.0

LANE = 128
TOKEN_BLOCK = 256
VMEM_LIMIT = 56 * 1024 * 1024

RET_OFF, RWKV_OFF, GLA_OFF, MLSTM_OFF, D_IN = 0, 1536, 3232, 4784, 6328
RET_W, RWKV_W, GLA_W, MLSTM_W = 1536, 1792, 1664, 1664


def _ret_cols():
    def half_split(base):
        lo = [base + h * 64 + j for h in range(4) for j in range(32)]
        hi = [base + h * 64 + 32 + j for h in range(4) for j in range(32)]
        return lo + hi
    cols = half_split(RET_OFF) + half_split(RET_OFF + 256)
    cols += list(range(RET_OFF + 512, RET_OFF + 1536))
    return np.asarray(cols, np.int32)


def _rwkv_cols():
    cols = list(range(RWKV_OFF, RWKV_OFF + 1696)) + [D_IN] * 96
    return np.asarray(cols, np.int32)


def _gla_cols():
    o = GLA_OFF
    cols = list(range(o, o + 1024)) + list(range(o + 1040, o + 1552)) + list(range(o + 1024, o + 1040)) + [D_IN] * 112
    return np.asarray(cols, np.int32)


def _mlstm_cols():
    o = MLSTM_OFF
    cols = list(range(o, o + 1024)) + list(range(o + 1032, o + 1544)) + list(range(o + 1024, o + 1032)) + [D_IN] * 120
    return np.asarray(cols, np.int32)


def _dot(a, b):
    return jnp.dot(a.astype(BF16), b.astype(BF16), preferred_element_type=F32)


def _dot_nt(a, b):
    return lax.dot_general(a.astype(BF16), b.astype(BF16), (((1,), (1,)), ((), ())), preferred_element_type=F32)


def _dot_tn(a, b):
    return lax.dot_general(a.astype(BF16), b.astype(BF16), (((0,), (0,)), ((), ())), preferred_element_type=F32)


def _split(a, terms):
    pieces, rem = [], a
    for t in range(terms):
        piece = rem.astype(BF16)
        pieces.append(piece)
        if t + 1 < terms:
            rem = rem - piece.astype(F32)
    return pieces


def _dot_sel(a, sel, terms):
    out = None
    for piece in _split(a, terms):
        d = jnp.dot(piece, sel, preferred_element_type=F32)
        out = d if out is None else out + d
    return out


def _sel_dot(sel, b, terms):
    out = None
    for piece in _split(b, terms):
        d = jnp.dot(sel, piece, preferred_element_type=F32)
        out = d if out is None else out + d
    return out


def _silu(x):
    return x * jax.nn.sigmoid(x)


def _head_norm_lanes(o, width):
    parts = []
    for h in range(o.shape[1] // width):
        oh = o[:, h * width:(h + 1) * width]
        mu = jnp.mean(oh, axis=-1, keepdims=True)
        d = oh - mu
        var = jnp.mean(d * d, axis=-1, keepdims=True)
        parts.append(d * lax.rsqrt(var + NORM_EPS))
    return jnp.concatenate(parts, axis=1)


RET_L = 128


def _ret_tables():
    lg = np.log(1.0 - 2.0 ** (-5.0 - np.arange(4, dtype=np.float64)))
    i = np.arange(RET_L, dtype=np.float64)
    lane = np.arange(256)
    head_of_lane = (lane % 128) // 32
    ms = np.zeros((4 * RET_L, 256), np.float32)
    mq = np.zeros((4 * RET_L, 256), np.float32)
    dk = np.zeros((4 * RET_L, 256), np.float32)
    di = np.zeros((4 * RET_L, RET_L), np.float32)
    for h in range(4):
        m = (head_of_lane == h).astype(np.float64)
        ms[h * RET_L:(h + 1) * RET_L] = m[None, :]
        mq[h * RET_L:(h + 1) * RET_L] = np.exp((i + 1.0) * lg[h])[:, None] * m[None, :]
        dk[h * RET_L:(h + 1) * RET_L] = np.exp((RET_L - 1.0 - i) * lg[h])[:, None]
        rel = i[:, None] - i[None, :]
        di[h * RET_L:(h + 1) * RET_L] = np.where(rel >= 0, np.exp(np.maximum(rel, 0.0) * lg[h]), 0.0)
    dchunk = np.exp(RET_L * lg).astype(np.float32)
    return ms, mq, dk, di, dchunk


def _ret_kernel(x_ref, w_ref, cos_ref, sin_ref, ms_ref, mq_ref, dk_ref, di_ref, g_ref, y_ref, s_ref, *, dchunk):
    @pl.when(pl.program_id(1) == 0)
    def _():
        s_ref[...] = jnp.zeros_like(s_ref)

    p = jnp.dot(x_ref[0], w_ref[...], preferred_element_type=F32)
    L = RET_L
    state = [s_ref[:, h * 128:(h + 1) * 128] for h in range(4)]
    for c in range(TOKEN_BLOCK // L):
        sl = slice(c * L, (c + 1) * L)
        q, k = p[sl, 0:256], p[sl, 256:512]
        v, gate = p[sl, 512:1024], p[sl, 1024:1536]
        cs, sn = cos_ref[sl, :], sin_ref[sl, :]

        def rope(t):
            t1, t2 = t[:, :128], t[:, 128:]
            return jnp.concatenate([t1 * cs - t2 * sn, t1 * sn + t2 * cs], axis=1)

        qr = rope(q) * (64.0 ** -0.5)
        kr = rope(k)
        q4 = jnp.concatenate([qr] * 4, axis=0)
        scores = _dot_nt(q4 * ms_ref[...], kr) * di_ref[...]
        qd = q4 * mq_ref[...]
        k4 = jnp.concatenate([kr] * 4, axis=0) * dk_ref[...]
        hbs = [slice(h * L, (h + 1) * L) for h in range(4)]
        vhs = [v[:, h * 128:(h + 1) * 128] for h in range(4)]
        inter = [_dot(qd[hbs[h]], state[h]) for h in range(4)]
        upd = [_dot_tn(k4[hbs[h]], vhs[h]) for h in range(4)]
        intra = [_dot(scores[hbs[h]], vhs[h]) for h in range(4)]
        outs = [intra[h] + inter[h] for h in range(4)]
        state = [dchunk[h] * state[h] + upd[h] for h in range(4)]
        o = _head_norm_lanes(jnp.concatenate(outs, axis=1), 128)
        y_ref[0, sl, :] = (o * g_ref[...] * _silu(gate)).astype(y_ref.dtype)
    for h in range(4):
        s_ref[:, h * 128:(h + 1) * 128] = state[h]


def _retention(xb, w, cos, sin, g, tables):
    B, T, _ = xb.shape
    ms, mq, dk, di, dchunk = tables
    const = lambda shape: pl.BlockSpec(shape, lambda b, t: (0,) * len(shape))
    return pl.pallas_call(
        functools.partial(_ret_kernel, dchunk=tuple(float(d) for d in dchunk)),
        grid=(B, T // TOKEN_BLOCK),
        in_specs=[
            pl.BlockSpec((1, TOKEN_BLOCK, D_MODEL), lambda b, t: (b, t, 0)),
            const((D_MODEL, RET_W)),
            pl.BlockSpec((TOKEN_BLOCK, 128), lambda b, t: (t, 0)),
            pl.BlockSpec((TOKEN_BLOCK, 128), lambda b, t: (t, 0)),
            const((4 * RET_L, 256)), const((4 * RET_L, 256)), const((4 * RET_L, 256)), const((4 * RET_L, RET_L)),
            const((1, GROUP)),
        ],
        out_specs=pl.BlockSpec((1, TOKEN_BLOCK, GROUP), lambda b, t: (b, t, 0)),
        out_shape=jax.ShapeDtypeStruct((B, T, GROUP), BF16),
        scratch_shapes=[pltpu.VMEM((256, GROUP), F32)],
        compiler_params=pltpu.CompilerParams(dimension_semantics=("arbitrary", "arbitrary"),
                                             vmem_limit_bytes=VMEM_LIMIT),
        name="retention",
    )(xb, w, cos, sin, jnp.asarray(ms), jnp.asarray(mq), jnp.asarray(dk), jnp.asarray(di), g)


RWKV_L = 64


RWKV_HG = 2
RWKV_GW = RWKV_HG * 64


def _rwkv_tables(n_batch):
    lane = np.arange(RWKV_GW)
    row = np.arange(RWKV_GW)
    ms4 = (lane[None, :] // 64 == row[:, None] // 64).astype(np.float32)
    same = (row[:, None] // 64 == row[None, :] // 64)
    tt, ii = row[:, None] % 64, row[None, :] % 64
    strict = (same & (tt > ii)).astype(np.float32)
    incl = (same & (tt >= ii)).astype(np.float32)
    rt = np.arange(n_batch * TOKEN_BLOCK)
    tri = ((rt[:, None] // 64 == rt[None, :] // 64) & (rt[:, None] >= rt[None, :])).astype(np.float32)
    r5 = np.arange(512)
    bd = (r5[:, None] // 64 == r5[None, :] // 64).astype(np.float32)
    return ms4, strict, incl, tri, bd


def _rwkv_kernel(x_ref, w_ref, mu_ref, w0_ref, w2_ref, a0_ref, a2_ref, g2_ref, kk_ref, ka_ref, rk_ref, gn_ref,
                 ms4_ref, strict_ref, incl_ref, tri_ref, bd_ref, y_ref, p_scr, s_scr):
    NB, TC, L = x_ref.shape[0], TOKEN_BLOCK, RWKV_L
    GW, NG = RWKV_GW, GROUP // RWKV_GW

    @pl.when(pl.program_id(0) == 0)
    def _():
        s_scr[...] = jnp.zeros_like(s_scr)
        for b in range(NB):
            p_scr[b, pl.ds(0, 8), :] = jnp.zeros((8, RWKV_W), F32)

    p = jnp.dot(x_ref[...].reshape(NB * TC, D_MODEL), w_ref[...], preferred_element_type=F32)
    prevs = []
    for b in range(NB):
        pb = p[b * TC:(b + 1) * TC]
        p_scr[b, pl.ds(8, TC), :] = pb
        prevs.append(p_scr[b, pl.ds(7, TC), :])
        p_scr[b, pl.ds(0, 8), :] = pb[TC - 8:, :]
    prev = jnp.concatenate(prevs, axis=0)
    cols = p + (prev - p) * mu_ref[...]
    r, k, v = cols[:, 0:512], cols[:, 512:1024], cols[:, 1024:1536]
    tail = cols[:, 1536:1792]
    w = -jax.nn.softplus(-(w0_ref[...] + _dot(jnp.tanh(tail), w2_ref[...]))) - 0.5
    logdec = -jnp.exp(w)
    a = jax.nn.sigmoid(a0_ref[...] + _dot(tail, a2_ref[...]))
    g = _dot(jax.nn.sigmoid(tail), g2_ref[...])
    kk = k * kk_ref[...]
    bd = bd_ref[...]
    kk = kk / jnp.maximum(jnp.sqrt(_dot_sel(kk * kk, bd, 1)), 1e-12)
    k = k * (1.0 + (a - 1.0) * ka_ref[...])
    bonus = _dot_sel(r * k * rk_ref[...], bd, 1) * v
    cum = _sel_dot(tri_ref[...], logdec, 2)
    beta = kk * a
    ms4 = ms4_ref[...]
    strict, incl = strict_ref[...], incl_ref[...]

    def stack(t):
        return jnp.concatenate([t] * RWKV_HG, axis=0) * ms4

    chains = [(b, gi) for b in range(NB) for gi in range(NG)]
    CH = range(len(chains))
    st = [s_scr[b, gi] for b, gi in chains]
    outs = [[[None] * NG for _ in range(TC // L)] for _ in range(NB)]
    for c in range(TC // L):
        x1, x2, yy, z1, z2, sv, cl = ([None] * len(chains) for _ in range(7))
        for i, (b, gi) in enumerate(chains):
            rows = slice(b * TC + c * L, b * TC + (c + 1) * L)
            lanes = slice(gi * GW, (gi + 1) * GW)
            cm, lw = cum[rows, lanes], logdec[rows, lanes]
            cl[i] = cm[L - 1:L, :]
            gam, gam_ex, igam, gam_r = jnp.exp(cm), jnp.exp(cm - lw), jnp.exp(-cm), jnp.exp(cl[i] - cm)
            kk_c, r_c, k_c, v_c, b_c = kk[rows, lanes], r[rows, lanes], k[rows, lanes], v[rows, lanes], beta[rows, lanes]
            x1[i], x2[i] = stack(-kk_c * gam_ex), stack(r_c * gam)
            yy[i] = jnp.concatenate([stack(b_c * igam), stack(k_c * igam)], axis=0)
            z1[i], z2[i] = stack(b_c * gam_r), stack(k_c * gam_r)
            sv[i] = stack(v_c)
        ab = [_dot_nt(jnp.concatenate([x1[i], x2[i]], axis=0), yy[i]) for i in CH]
        a_b = [ab[i][0:GW, 0:GW] * strict for i in CH]
        a_k = [ab[i][0:GW, GW:2 * GW] * strict for i in CH]
        b_b = [ab[i][GW:2 * GW, 0:GW] * incl for i in CH]
        b_k = [ab[i][GW:2 * GW, GW:2 * GW] * incl for i in CH]
        u = [_dot(a_k[i], sv[i]) for i in CH]
        t0 = [_dot_nt(x1[i], st[i]) for i in CH]
        u = [u[i] + t0[i] for i in CH]
        pw = a_b
        for it in range(6):
            t = [_dot(pw[i], u[i]) for i in CH]
            if it < 5:
                pw = [_dot(pw[i], pw[i]) for i in CH]
            u = [u[i] + t[i] for i in CH]
        y0 = [_dot_nt(x2[i], st[i]) for i in CH]
        y1 = [_dot(b_b[i], u[i]) for i in CH]
        y2 = [_dot(b_k[i], sv[i]) for i in CH]
        s1 = [_dot_tn(u[i], z1[i]) for i in CH]
        s2 = [_dot_tn(sv[i], z2[i]) for i in CH]
        for i, (b, gi) in enumerate(chains):
            yst = y0[i] + y1[i] + y2[i]
            outs[b][c][gi] = sum(yst[h * L:(h + 1) * L] for h in range(RWKV_HG))
            st[i] = st[i] * jnp.exp(cl[i]) + s1[i] + s2[i]
    for i, (b, gi) in enumerate(chains):
        s_scr[b, gi] = st[i]

    y = jnp.concatenate([jnp.concatenate(o, axis=1) for ob in outs for o in ob], axis=0)
    mean = _dot_sel(y, bd, 1) * (1.0 / 64.0)
    d = y - mean
    var = _dot_sel(d * d, bd, 1) * (1.0 / 64.0)
    yn = d * lax.rsqrt(var + NORM_EPS) * gn_ref[...]
    y_ref[...] = ((yn + bonus) * g).astype(y_ref.dtype).reshape(NB, TC, GROUP)


def _rwkv(xb, w, mu, w0, w2p, a0, a2p, g2p, k_k, k_a, r_k, gn, tables):
    B, T, _ = xb.shape
    const = lambda shape: pl.BlockSpec(shape, lambda t: (0,) * len(shape))
    row = lambda n: const((1, n))
    return pl.pallas_call(
        _rwkv_kernel,
        grid=(T // TOKEN_BLOCK,),
        in_specs=[
            pl.BlockSpec((B, TOKEN_BLOCK, D_MODEL), lambda t: (0, t, 0)),
            const((D_MODEL, RWKV_W)), row(RWKV_W), row(GROUP), const((256, GROUP)), row(GROUP), const((256, GROUP)),
            const((256, GROUP)), row(GROUP), row(GROUP), row(GROUP), row(GROUP),
            const((RWKV_GW, RWKV_GW)), const((RWKV_GW, RWKV_GW)), const((RWKV_GW, RWKV_GW)),
            const((B * TOKEN_BLOCK, B * TOKEN_BLOCK)), const((GROUP, GROUP)),
        ],
        out_specs=pl.BlockSpec((B, TOKEN_BLOCK, GROUP), lambda t: (0, t, 0)),
        out_shape=jax.ShapeDtypeStruct((B, T, GROUP), BF16),
        scratch_shapes=[pltpu.VMEM((B, TOKEN_BLOCK + 8, RWKV_W), F32),
                        pltpu.VMEM((B, GROUP // RWKV_GW, RWKV_GW, RWKV_GW), F32)],
        compiler_params=pltpu.CompilerParams(dimension_semantics=("arbitrary",), vmem_limit_bytes=VMEM_LIMIT),
        name="rwkv7",
    )(xb, w, mu, w0, w2p, a0, a2p, g2p, k_k, k_a, r_k, gn,
      *[jnp.asarray(t, dt) for t, dt in zip(tables(B), (F32, F32, F32, BF16, BF16))])


GLA_SUB = 16


def _gla_tables():
    r = np.arange(256)
    tri16 = ((r[:, None] // 16 == r[None, :] // 16) & (r[:, None] >= r[None, :])).astype(np.float32)
    sel = (r[:, None] // 64 == np.arange(512)[None, :] // 128).astype(np.float32)
    ms = (np.arange(256)[None, :] // 64 == np.arange(64)[:, None] // 16).astype(np.float32)
    return tri16, sel, ms


def _gla_kernel(x_ref, w_ref, g2_ref, gb_ref, gn_ref, tri_ref, sel_ref, ms_ref, y_ref, st_scr):
    TC, SUB = TOKEN_BLOCK, GLA_SUB

    @pl.when(pl.program_id(1) == 0)
    def _():
        st_scr[...] = jnp.zeros_like(st_scr)

    p = jnp.dot(x_ref[0], w_ref[...], preferred_element_type=F32)
    gate = p[:, 1024:1536]
    la = jax.nn.log_sigmoid(_dot(p[:, 1536:1664], g2_ref[...]) + gb_ref[...]) * (1.0 / GLA_TAU)
    q, k, v = p[:, 0:256] * (64.0 ** -0.5), p[:, 256:512], p[:, 512:1024]
    b = _sel_dot(tri_ref[...], la, 3)
    sel = sel_ref[...]
    ms = ms_ref[...]
    row = lax.broadcasted_iota(jnp.int32, (SUB, 1), 0)
    outs = []
    st = st_scr[...]
    for s in range(TC // SUB):
        rows = slice(s * SUB, (s + 1) * SUB)
        qb, kb, vb, bb = q[rows], k[rows], v[rows], b[rows]
        bl = bb[SUB - 1:SUB, :]
        qe = jnp.concatenate([qb * jnp.exp(bb)] * 4, axis=0) * ms
        o_inter = jnp.concatenate(
            [_dot_nt(qe[h * SUB:(h + 1) * SUB], st[h * 128:(h + 1) * 128, :]) for h in range(4)], axis=1)
        ts = []
        for j in range(SUB):
            diff = jnp.where(row >= j, bb - bb[j:j + 1, :], -1e30)
            ts.append(qb * kb[j:j + 1, :] * jnp.exp(diff))
        rr = jnp.dot(jnp.concatenate(ts, axis=0).astype(BF16), sel, preferred_element_type=F32)
        o_intra = rr[0:SUB] * vb[0:1, :]
        for j in range(1, SUB):
            o_intra = o_intra + rr[j * SUB:(j + 1) * SUB] * vb[j:j + 1, :]
        outs.append(o_intra + o_inter)
        st = st * jnp.exp(bl) + _dot_tn(vb, kb * jnp.exp(bl - bb))
    st_scr[...] = st
    o = _head_norm_lanes(jnp.concatenate(outs, axis=0), 128)
    y_ref[0] = (o * gn_ref[...] * _silu(gate)).astype(y_ref.dtype)


def _gla(xb, w, g2p, gb, gn, tables):
    B, T, _ = xb.shape
    const = lambda shape: pl.BlockSpec(shape, lambda b, t: (0,) * len(shape))
    return pl.pallas_call(
        _gla_kernel,
        grid=(B, T // TOKEN_BLOCK),
        in_specs=[
            pl.BlockSpec((1, TOKEN_BLOCK, D_MODEL), lambda b, t: (b, t, 0)),
            const((D_MODEL, GLA_W)), const((128, 256)), const((1, 256)), const((1, GROUP)),
            const((256, 256)), const((256, 512)), const((64, 256)),
        ],
        out_specs=pl.BlockSpec((1, TOKEN_BLOCK, GROUP), lambda b, t: (b, t, 0)),
        out_shape=jax.ShapeDtypeStruct((B, T, GROUP), BF16),
        scratch_shapes=[pltpu.VMEM((GROUP, 256), F32)],
        compiler_params=pltpu.CompilerParams(dimension_semantics=("arbitrary", "arbitrary"),
                                             vmem_limit_bytes=VMEM_LIMIT),
        name="gla",
    )(xb, w, g2p, gb, gn, *[jnp.asarray(t, dt) for t, dt in zip(tables, (BF16, BF16, F32))])


MLSTM_L = 128


def _mlstm_tables():
    r = np.arange(TOKEN_BLOCK)
    tri = ((r[:, None] // MLSTM_L == r[None, :] // MLSTM_L) & (r[:, None] >= r[None, :])).astype(np.float32)
    ms = (np.arange(256)[None, :] // 64 == np.arange(4 * MLSTM_L)[:, None] // MLSTM_L).astype(np.float32)
    return tri, ms


def _mlstm_kernel(x_ref, w_ref, conv_ref, gbias_ref, gn_ref, tri_ref, ms_ref, y_ref,
                  qk_scr, c_scr, n_scr, m_scr):
    TC, L = TOKEN_BLOCK, MLSTM_L

    @pl.when(pl.program_id(1) == 0)
    def _():
        c_scr[...] = jnp.zeros_like(c_scr)
        n_scr[...] = jnp.zeros_like(n_scr)
        m_scr[...] = jnp.zeros_like(m_scr)
        qk_scr[pl.ds(0, 8), :] = jnp.zeros((8, 512), F32)

    p = jnp.dot(x_ref[0], w_ref[...], preferred_element_type=F32)
    qk_scr[pl.ds(8, TC), :] = p[:, 0:512]
    conv = p[:, 0:512] * conv_ref[3:4, :]
    for j in range(3):
        conv = conv + qk_scr[pl.ds(5 + j, TC), :] * conv_ref[j:j + 1, :]
    qk_scr[pl.ds(0, 8), :] = p[TC - 8:, 0:512]
    qk = _silu(conv)
    q, k = qk[:, 0:256], qk[:, 256:512] * (64.0 ** -0.5)
    v, og = p[:, 512:1024], p[:, 1024:1536]
    tail = p[:, 1536:1664] + gbias_ref[...]
    lane = lax.broadcasted_iota(jnp.int32, (1, 128), 1)
    logf = jnp.where((lane >= 4) & (lane < 8), jax.nn.log_sigmoid(tail), 0.0)
    cumf = _sel_dot(tri_ref[...], logf, 3)
    gcol = jnp.where(lane < 4, tail, cumf)
    eye = (lax.broadcasted_iota(jnp.int32, (128, 128), 0) == lax.broadcasted_iota(jnp.int32, (128, 128), 1)).astype(BF16)
    ms = ms_ref[...]
    ri = lax.broadcasted_iota(jnp.int32, (L, L), 0)
    ci = lax.broadcasted_iota(jnp.int32, (L, L), 1)
    causal = ri >= ci
    lane256 = lax.broadcasted_iota(jnp.int32, (1, 256), 1)
    c_state = [c_scr[:, h * 128:(h + 1) * 128] for h in range(4)]
    m_state = [m_scr[0:1, h:h + 1] for h in range(4)]
    n_state = n_scr[...]

    for c in range(TC // L):
        rows = slice(c * L, (c + 1) * L)
        gc = gcol[rows, :]
        gr = sum(lax.dot_general(eye, piece, (((1,), (1,)), ((), ())), preferred_element_type=F32)
                 for piece in _split(gc, 3))
        qc, kc, vc = q[rows], k[rows], v[rows]
        q4 = jnp.concatenate([qc] * 4, axis=0) * ms
        scores = _dot_nt(q4, kc)
        qn = jnp.sum(q4 * n_state, axis=1, keepdims=True)
        hbs = [slice(h * L, (h + 1) * L) for h in range(4)]
        vhs = [vc[:, h * 128:(h + 1) * 128] for h in range(4)]
        qc_state = [_dot(q4[hbs[h]], c_state[h]) for h in range(4)]
        ss, inters, m_rows, wj_cols, wrows, scales = [], [], [], [], [], []
        n_scale = jnp.zeros((1, 256), F32)
        for h in range(4):
            b_col, i_col = gc[:, 4 + h:5 + h], gc[:, h:h + 1]
            b_row, i_row = gr[4 + h:5 + h, :], gr[h:h + 1, :]
            m_prev = m_state[h]
            log_d = jnp.where(causal, b_col - b_row + i_row, -jnp.inf)
            m_inter = b_col + m_prev
            m_row = jnp.maximum(jnp.max(log_d, axis=-1, keepdims=True), m_inter)
            ss.append(scores[hbs[h]] * jnp.exp(log_d - m_row))
            inters.append(jnp.exp(m_inter - m_row))
            m_rows.append(m_row)
            m_new = m_row[L - 1:L, :]
            b_last = b_col[L - 1:L, :]
            wj_cols.append(jnp.exp(b_last - b_col + i_col - m_new))
            wrows.append(jnp.exp(b_last - b_row + i_row - m_new))
            scale = jnp.exp(b_last + m_prev - m_new)
            scales.append(scale)
            n_scale = jnp.where(lane256 // 64 == h, scale, n_scale)
            m_state[h] = m_new
        sv = [_dot(ss[h], vhs[h]) for h in range(4)]
        upd = [_dot_tn(kc * wj_cols[h], vhs[h]) for h in range(4)]
        outs = []
        for h in range(4):
            num = sv[h] + inters[h] * qc_state[h]
            den = jnp.sum(ss[h], axis=-1, keepdims=True) + inters[h] * qn[hbs[h]]
            outs.append(num / jnp.maximum(jnp.abs(den), jnp.exp(-m_rows[h])))
            c_state[h] = scales[h] * c_state[h] + upd[h]
        wk = _dot(jnp.concatenate(wrows + [jnp.zeros((4, L), F32)], axis=0), kc)
        n_add = jnp.zeros((1, 256), F32)
        for h in range(4):
            n_add = jnp.where(lane256 // 64 == h, wk[h:h + 1, :], n_add)
        n_state = n_scale * n_state + n_add
        o = _head_norm_lanes(jnp.concatenate(outs, axis=1), 128)
        y_ref[0, rows, :] = (o * gn_ref[...] * jax.nn.sigmoid(og[rows])).astype(y_ref.dtype)

    n_scr[...] = n_state
    for h in range(4):
        c_scr[:, h * 128:(h + 1) * 128] = c_state[h]
        m_scr[0:1, h:h + 1] = m_state[h]


def _mlstm(xb, w, conv_w, gbias, gn, tables):
    B, T, _ = xb.shape
    const = lambda shape: pl.BlockSpec(shape, lambda b, t: (0,) * len(shape))
    return pl.pallas_call(
        _mlstm_kernel,
        grid=(B, T // TOKEN_BLOCK),
        in_specs=[
            pl.BlockSpec((1, TOKEN_BLOCK, D_MODEL), lambda b, t: (b, t, 0)),
            const((D_MODEL, MLSTM_W)), const((4, 512)), const((1, 128)), const((1, GROUP)),
            const((TOKEN_BLOCK, TOKEN_BLOCK)), const((4 * MLSTM_L, 256)),
        ],
        out_specs=pl.BlockSpec((1, TOKEN_BLOCK, GROUP), lambda b, t: (b, t, 0)),
        out_shape=jax.ShapeDtypeStruct((B, T, GROUP), BF16),
        scratch_shapes=[pltpu.VMEM((TOKEN_BLOCK + 8, 512), F32), pltpu.VMEM((256, GROUP), F32),
                        pltpu.VMEM((1, 256), F32), pltpu.VMEM((8, 128), F32)],
        compiler_params=pltpu.CompilerParams(dimension_semantics=("arbitrary", "arbitrary"),
                                             vmem_limit_bytes=VMEM_LIMIT),
        name="mlstm",
    )(xb, w, conv_w, gbias, gn, *[jnp.asarray(t, dt) for t, dt in zip(tables, (BF16, F32))])


ROW_BLOCK = 512


def _layer_norm(z, g, b):
    mu = jnp.mean(z, axis=-1, keepdims=True)
    d = z - mu
    var = jnp.mean(d * d, axis=-1, keepdims=True)
    return d * lax.rsqrt(var + NORM_EPS) * g + b


def _outproj_kernel(y0_ref, y1_ref, y2_ref, y3_ref, x_ref, w_ref, g_ref, b_ref, rwh_ref, rwl_ref, rb_ref,
                    x1_ref, x1b_ref, idx_ref, gate_ref):
    acc = jnp.dot(y0_ref[...], w_ref[0:512, :], preferred_element_type=F32)
    acc += jnp.dot(y1_ref[...], w_ref[512:1024, :], preferred_element_type=F32)
    acc += jnp.dot(y2_ref[...], w_ref[1024:1536, :], preferred_element_type=F32)
    acc += jnp.dot(y3_ref[...], w_ref[1536:2048, :], preferred_element_type=F32)
    x1 = _layer_norm(DN_ALPHA * x_ref[...] + acc, g_ref[...], b_ref[...])
    x1_ref[...] = x1
    x1b_ref[...] = x1.astype(BF16)
    x_hi, x_lo = _split(x1, 2)
    logits = (jnp.dot(x_hi, rwh_ref[...], preferred_element_type=F32)
              + jnp.dot(x_lo, rwh_ref[...], preferred_element_type=F32)
              + jnp.dot(x_hi, rwl_ref[...], preferred_element_type=F32)) + rb_ref[...]
    lane = lax.broadcasted_iota(jnp.int32, logits.shape, 1)
    logits = jnp.where(lane < N_EXPERTS, logits, -jnp.inf)
    idx_out = jnp.zeros(logits.shape, jnp.int32)
    gate_out = jnp.zeros(logits.shape, F32)
    top = None
    for kth in range(TOP_K):
        m = jnp.max(logits, axis=-1, keepdims=True)
        sel = jnp.min(jnp.where(logits == m, lane, LANE), axis=-1, keepdims=True)
        if kth == 0:
            top = m
        idx_out = jnp.where(lane == kth, sel, idx_out)
        gate_out = jnp.where(lane == kth, jnp.exp(m - top), gate_out)
        logits = jnp.where(lane == sel, -jnp.inf, logits)
    gate_out = gate_out / jnp.sum(gate_out, axis=-1, keepdims=True)
    idx_ref[...] = idx_out
    gate_ref[...] = gate_out


def _outproj(ys, x, w_out, g, b, rw, rb):
    N = x.shape[0]
    rb_ = ROW_BLOCK
    const = lambda shape: pl.BlockSpec(shape, lambda i: (0,) * len(shape))
    rows = lambda width: pl.BlockSpec((rb_, width), lambda i: (i, 0))
    rw_hi = rw.astype(BF16)
    rw_lo = (rw - rw_hi.astype(F32)).astype(BF16)
    return pl.pallas_call(
        _outproj_kernel,
        grid=(N // rb_,),
        in_specs=[rows(GROUP)] * 4 + [rows(D_MODEL), const((D_MODEL, D_MODEL)), const((1, D_MODEL)), const((1, D_MODEL)),
                                      const((D_MODEL, LANE)), const((D_MODEL, LANE)), const((1, LANE))],
        out_specs=[rows(D_MODEL), rows(D_MODEL), rows(LANE), rows(LANE)],
        out_shape=[jax.ShapeDtypeStruct((N, D_MODEL), F32), jax.ShapeDtypeStruct((N, D_MODEL), BF16),
                   jax.ShapeDtypeStruct((N, LANE), jnp.int32), jax.ShapeDtypeStruct((N, LANE), F32)],
        compiler_params=pltpu.CompilerParams(dimension_semantics=("arbitrary",), vmem_limit_bytes=VMEM_LIMIT),
        name="outproj_norm_router",
    )(*ys, x, w_out, g, b, rw_hi, rw_lo, rb)


MOE_BLOCK = 512
FF_TILE = 1024
DN_TILE = 1024


def _count_le(bounds, q):
    return jnp.sum((q[:, None] >= bounds[None, :]).astype(jnp.int32), axis=1)


def _ffn_schedule(pcounts, pstarts, n_blocks, n_tiles):
    nblk = (pcounts // MOE_BLOCK).astype(jnp.int32)
    bstart = (pstarts // MOE_BLOCK).astype(jnp.int32)
    steps_e = nblk * n_tiles
    ends = jnp.cumsum(steps_e).astype(jnp.int32)
    begins = ends - steps_e
    total = ends[-1]
    s = jnp.arange(n_blocks * n_tiles, dtype=jnp.int32)
    e = jnp.minimum(_count_le(ends, s), N_EXPERTS - 1)
    r = s - begins[e]
    nb_e = jnp.maximum(nblk[e], 1)
    til, j = r // nb_e, r % nb_e
    blk = bstart[e] + j
    last = jnp.maximum(total - 1, 0)
    valid = s < total
    rp = s - total
    blk = jnp.where(valid, blk, total // n_tiles + rp // n_tiles)
    otil = jnp.where(valid, til, rp % n_tiles)
    wtil = jnp.where(valid, til, til[last])
    e = jnp.where(valid, e, e[last])
    first = (valid & (j == 0)).astype(jnp.int32)
    i32 = lambda a: a.astype(jnp.int32)
    return i32(blk), i32(otil), i32(wtil), i32(e), first, i32(total).reshape(1)


def _ffn_up_kernel(blk_ref, otil_ref, wtil_ref, ex_ref, first_ref, ns_ref,
                   xs_ref, wg_ref, wl_ref, bg_ref, bl_ref, act_ref, wg_bf, wl_bf):
    s = pl.program_id(0)

    @pl.when(first_ref[s] == 1)
    def _():
        wg_bf[...] = wg_ref[...].astype(BF16)
        wl_bf[...] = wl_ref[...].astype(BF16)

    @pl.when(s < ns_ref[0])
    def _():
        xs = xs_ref[...]
        glu = jnp.dot(xs, wg_bf[...], preferred_element_type=F32) + bg_ref[...]
        lin = jnp.dot(xs, wl_bf[...], preferred_element_type=F32) + bl_ref[...]
        glu = jnp.minimum(glu, SWIGLU_LIMIT)
        lin = jnp.clip(lin, -SWIGLU_LIMIT, SWIGLU_LIMIT)
        act_ref[...] = (glu * jax.nn.sigmoid(SWIGLU_ALPHA * glu) * (lin + 1.0)).astype(act_ref.dtype)

    @pl.when(s >= ns_ref[0])
    def _():
        act_ref[...] = jnp.zeros_like(act_ref)


def _ffn_dn_kernel(blk_ref, otil_ref, wtil_ref, ex_ref, first_ref, ns_ref,
                   act_ref, wd_ref, bd_ref, gate_ref, o_ref, wd_bf):
    s = pl.program_id(0)

    @pl.when(first_ref[s] == 1)
    def _():
        wd_bf[...] = wd_ref[...].astype(BF16)

    @pl.when(s < ns_ref[0])
    def _():
        out = jnp.dot(act_ref[...], wd_bf[...], preferred_element_type=F32) + bd_ref[...]
        o_ref[...] = (out * gate_ref[...]).astype(o_ref.dtype)

    @pl.when(s >= ns_ref[0])
    def _():
        o_ref[...] = jnp.zeros_like(o_ref)


def _expert_ffn(layer, xs, sched_up, sched_dn, w_gu, b_gu, w_dn, b_dn, slot_gate):
    n_slots = xs.shape[0]
    nt = D_FF // FF_TILE
    n_steps = (n_slots // MOE_BLOCK) * nt
    n_steps_dn = (n_slots // MOE_BLOCK) * (D_MODEL // DN_TILE)
    params = pltpu.CompilerParams(dimension_semantics=("arbitrary",), vmem_limit_bytes=VMEM_LIMIT)
    wtile = lambda off: (lambda s, blk, otil, wtil, ex, first, ns: (layer, ex[s], 0, off + wtil[s]))
    rows_full = lambda s, blk, otil, wtil, ex, first, ns: (blk[s], 0)
    rows_tile = lambda s, blk, otil, wtil, ex, first, ns: (blk[s], otil[s])
    act = pl.pallas_call(
        _ffn_up_kernel,
        grid_spec=pltpu.PrefetchScalarGridSpec(
            num_scalar_prefetch=6,
            grid=(n_steps,),
            in_specs=[
                pl.BlockSpec((MOE_BLOCK, D_MODEL), rows_full),
                pl.BlockSpec((None, None, D_MODEL, FF_TILE), wtile(0)),
                pl.BlockSpec((None, None, D_MODEL, FF_TILE), wtile(nt)),
                pl.BlockSpec((None, None, 1, FF_TILE), wtile(0)),
                pl.BlockSpec((None, None, 1, FF_TILE), wtile(nt)),
            ],
            out_specs=pl.BlockSpec((MOE_BLOCK, FF_TILE), rows_tile),
            scratch_shapes=[pltpu.VMEM((D_MODEL, FF_TILE), BF16), pltpu.VMEM((D_MODEL, FF_TILE), BF16)],
        ),
        out_shape=jax.ShapeDtypeStruct((n_slots, D_FF), BF16),
        compiler_params=params,
        name="expert_up",
    )(*sched_up, xs, w_gu, w_gu, b_gu, b_gu)
    return pl.pallas_call(
        _ffn_dn_kernel,
        grid_spec=pltpu.PrefetchScalarGridSpec(
            num_scalar_prefetch=6,
            grid=(n_steps_dn,),
            in_specs=[
                pl.BlockSpec((MOE_BLOCK, D_FF), rows_full),
                pl.BlockSpec((None, None, D_FF, DN_TILE), wtile(0)),
                pl.BlockSpec((None, None, 1, DN_TILE), wtile(0)),
                pl.BlockSpec((MOE_BLOCK, 1), rows_full),
            ],
            out_specs=pl.BlockSpec((MOE_BLOCK, DN_TILE), rows_tile),
            scratch_shapes=[pltpu.VMEM((D_FF, DN_TILE), BF16)],
        ),
        out_shape=jax.ShapeDtypeStruct((n_slots, D_MODEL), BF16),
        compiler_params=params,
        name="expert_down",
    )(*sched_dn, act, w_dn, b_dn, slot_gate)


def _combine_kernel(x_ref, y0_ref, y1_ref, y2_ref, y3_ref, g_ref, b_ref, o_ref, ob_ref):
    y = (y0_ref[...].astype(F32) + y1_ref[...].astype(F32)) + (y2_ref[...].astype(F32) + y3_ref[...].astype(F32))
    x2 = _layer_norm(DN_ALPHA * x_ref[...] + y, g_ref[...], b_ref[...])
    o_ref[...] = x2
    ob_ref[...] = x2.astype(BF16)


def _combine(x1, ysg, g, b):
    N = x1.shape[0]
    rb_ = ROW_BLOCK
    const = lambda shape: pl.BlockSpec(shape, lambda i: (0,) * len(shape))
    rows = pl.BlockSpec((rb_, D_MODEL), lambda i: (i, 0))
    return pl.pallas_call(
        _combine_kernel,
        grid=(N // rb_,),
        in_specs=[rows] * (1 + TOP_K) + [const((1, D_MODEL)), const((1, D_MODEL))],
        out_specs=[rows] * 2,
        out_shape=[jax.ShapeDtypeStruct((N, D_MODEL), F32), jax.ShapeDtypeStruct((N, D_MODEL), BF16)],
        compiler_params=pltpu.CompilerParams(dimension_semantics=("arbitrary",), vmem_limit_bytes=VMEM_LIMIT),
        name="combine_norm",
    )(x1, *ysg, g, b)


def _route(top_idx, gate):
    n_tok = top_idx.shape[0]
    n_assign = n_tok * TOP_K
    flat_e = top_idx.reshape(-1).astype(jnp.int32)
    flat_gate = gate.reshape(-1)
    iota = jnp.arange(n_assign, dtype=jnp.int32)
    experts = jnp.arange(N_EXPERTS, dtype=jnp.int32)
    counts = jnp.sum((flat_e[:, None] == experts[None, :]).astype(jnp.int32), axis=0)
    starts = jnp.cumsum(counts) - counts
    pcounts = (counts + MOE_BLOCK - 1) // MOE_BLOCK * MOE_BLOCK
    pends = jnp.cumsum(pcounts)
    pstarts = pends - pcounts
    _, order = lax.sort((flat_e, iota), num_keys=1)
    shift = jnp.sum(jnp.where(iota[:, None] >= (starts + counts)[None, :], (pcounts - counts)[None, :], 0), axis=1)
    dest = (iota + shift).astype(jnp.int32)
    n_slots = -(-n_assign // MOE_BLOCK) * MOE_BLOCK + N_EXPERTS * MOE_BLOCK
    n_blocks = n_slots // MOE_BLOCK
    block_start = jnp.arange(n_blocks, dtype=jnp.int32) * MOE_BLOCK
    block_exp = jnp.minimum(_count_le(pends, block_start), N_EXPERTS - 1)
    boff = block_start - pstarts[block_exp]
    first_a = starts[block_exp] + boff
    n_valid = jnp.clip(counts[block_exp] - boff, 0, MOE_BLOCK)
    j = jnp.arange(MOE_BLOCK, dtype=jnp.int32)
    valid = j[None, :] < n_valid[:, None]
    src = order[jnp.clip(first_a[:, None] + j[None, :], 0, n_assign - 1)]
    slot_tok = jnp.where(valid, src // TOP_K, 0).astype(jnp.int32).reshape(-1)
    slot_gate = jnp.where(valid, flat_gate[src], 0.0).reshape(-1, 1)
    _, pos = lax.sort((order, dest), num_keys=1)
    sched_up = _ffn_schedule(pcounts, pstarts, n_blocks, D_FF // FF_TILE)
    sched_dn = _ffn_schedule(pcounts, pstarts, n_blocks, D_MODEL // DN_TILE)
    return slot_tok, slot_gate, pos, sched_up, sched_dn


def _pad_rows(w, row0, total):
    return jnp.zeros((total, w.shape[1]), F32).at[row0:row0 + w.shape[0]].set(w)


def kernel(x, w_in, rwkv_mu, rwkv_w0, rwkv_w2, rwkv_a0, rwkv_a2, rwkv_g2, rwkv_k_k, rwkv_k_a, rwkv_r_k, gla_g2, gla_gb, mlstm_conv, mlstm_ib, mlstm_fb, head_norm_g, w_out, ln1_g, ln1_b, router_w, router_b, w_gu, b_gu, w_dn, b_dn, ln2_g, ln2_b):
    B, T, D = x.shape
    N = B * T
    depth = w_in.shape[0]
    ret_tab, rwkv_tab, gla_tab, mlstm_tab = _ret_tables(), _rwkv_tables, _gla_tables(), _mlstm_tables()
    pos_t = jnp.arange(T, dtype=F32)
    inv = ROPE_BASE ** (-jnp.arange(32, dtype=F32) / 32)
    ang = pos_t[:, None] * inv[None, :]
    cos = jnp.tile(jnp.cos(ang), (1, 4))
    sin = jnp.tile(jnp.sin(ang), (1, 4))
    row = lambda v: v.reshape(1, -1)

    b_gu4 = b_gu.reshape(depth, N_EXPERTS, 1, 2 * D_FF)
    b_dn4 = b_dn.reshape(depth, N_EXPERTS, 1, D_MODEL)
    take_rows = lambda a, i: a.at[i].get(mode='promise_in_bounds')

    xf = x.reshape(N, D)
    xb = x.astype(BF16)
    for l in range(depth):
        w_ext = jnp.concatenate([w_in[l], jnp.zeros((D, 1), F32)], axis=1)
        seg = lambda cols: jnp.take(w_ext, jnp.asarray(cols), axis=1).astype(BF16)
        gn = head_norm_g[l]
        y_ret = _retention(xb, seg(_ret_cols()), cos, sin, row(gn[0:512]), ret_tab)
        mu = jnp.concatenate([rwkv_mu[l], jnp.zeros((96,), F32)])
        y_rwkv = _rwkv(xb, seg(_rwkv_cols()), row(mu), row(rwkv_w0[l]), _pad_rows(rwkv_w2[l], 0, 256),
                       row(rwkv_a0[l]), _pad_rows(rwkv_a2[l], 32, 256), _pad_rows(rwkv_g2[l], 64, 256),
                       row(rwkv_k_k[l]), row(rwkv_k_a[l]), row(rwkv_r_k[l]), row(gn[512:1024]), rwkv_tab)
        y_gla = _gla(xb, seg(_gla_cols()), _pad_rows(gla_g2[l], 0, 128), row(gla_gb[l]), row(gn[1024:1536]), gla_tab)
        gbias = jnp.concatenate([mlstm_ib[l], mlstm_fb[l], jnp.zeros((120,), F32)])
        y_mlstm = _mlstm(xb, seg(_mlstm_cols()), mlstm_conv[l], row(gbias), row(gn[1536:2048]), mlstm_tab)
        ys = [y.reshape(N, GROUP) for y in (y_ret, y_rwkv, y_gla, y_mlstm)]
        rw = jnp.zeros((D, LANE), F32).at[:, :N_EXPERTS].set(router_w[l])
        rb = jnp.zeros((1, LANE), F32).at[0, :N_EXPERTS].set(router_b[l])
        x1, x1b, idx, gate = _outproj(ys, xf, w_out[l].astype(BF16), row(ln1_g[l]), row(ln1_b[l]), rw, rb)
        slot_tok, slot_gate, pos, sched_up, sched_dn = _route(idx[:, :TOP_K], gate[:, :TOP_K])
        xs = take_rows(x1b, slot_tok)
        ys_moe = _expert_ffn(l, xs, sched_up, sched_dn, w_gu, b_gu4, w_dn, b_dn4, slot_gate)
        pos = pos.reshape(N, TOP_K)
        ysg = [take_rows(ys_moe, pos[:, kth]) for kth in range(TOP_K)]
        xf, xb2 = _combine(x1, ysg, row(ln2_g[l]), row(ln2_b[l]))
        xb = xb2.reshape(B, T, D)
    return xf.reshape(B, T, D)
```

```python
import functools
import math

import numpy as np
import jax
import jax.numpy as jnp
from jax import lax
from jax.experimental import pallas as pl
from jax.experimental.pallas import tpu as pltpu

F32 = jnp.float32
BF16 = jnp.bfloat16
HI = lax.Precision.HIGHEST

D_MODEL = 2048
GROUP = 512
N_EXPERTS = 32
TOP_K = 4
D_FF = 2048
DEPTH = 2
NORM_EPS = 1e-5
DN_ALPHA = (2 * DEPTH) ** 0.25
ROPE_BASE = 10000.0
GLA_TAU = 16.0
SWIGLU_ALPHA = 1.702
SWIGLU_LIMIT = 7.0

LANE = 128
TOKEN_BLOCK = 256
VMEM_LIMIT = 56 * 1024 * 1024

RET_OFF, RWKV_OFF, GLA_OFF, MLSTM_OFF, D_IN = 0, 1536, 3232, 4784, 6328
RET_W, RWKV_W, GLA_W, MLSTM_W = 1536, 1792, 1664, 1664


def _ret_cols():
    def half_split(base):
        lo = [base + h * 64 + j for h in range(4) for j in range(32)]
        hi = [base + h * 64 + 32 + j for h in range(4) for j in range(32)]
        return lo + hi
    cols = half_split(RET_OFF) + half_split(RET_OFF + 256)
    cols += list(range(RET_OFF + 512, RET_OFF + 1536))
    return np.asarray(cols, np.int32)


def _rwkv_cols():
    cols = list(range(RWKV_OFF, RWKV_OFF + 1696)) + [D_IN] * 96
    return np.asarray(cols, np.int32)


def _gla_cols():
    o = GLA_OFF
    cols = list(range(o, o + 1024)) + list(range(o + 1040, o + 1552)) + list(range(o + 1024, o + 1040)) + [D_IN] * 112
    return np.asarray(cols, np.int32)


def _mlstm_cols():
    o = MLSTM_OFF
    cols = list(range(o, o + 1024)) + list(range(o + 1032, o + 1544)) + list(range(o + 1024, o + 1032)) + [D_IN] * 120
    return np.asarray(cols, np.int32)


def _dot(a, b):
    return jnp.dot(a.astype(BF16), b.astype(BF16), preferred_element_type=F32)


def _dot_nt(a, b):
    return lax.dot_general(a.astype(BF16), b.astype(BF16), (((1,), (1,)), ((), ())), preferred_element_type=F32)


def _dot_tn(a, b):
    return lax.dot_general(a.astype(BF16), b.astype(BF16), (((0,), (0,)), ((), ())), preferred_element_type=F32)


def _split(a, terms):
    pieces, rem = [], a
    for t in range(terms):
        piece = rem.astype(BF16)
        pieces.append(piece)
        if t + 1 < terms:
            rem = rem - piece.astype(F32)
    return pieces


def _dot_sel(a, sel, terms):
    out = None
    for piece in _split(a, terms):
        d = jnp.dot(piece, sel, preferred_element_type=F32)
        out = d if out is None else out + d
    return out


def _sel_dot(sel, b, terms):
    out = None
    for piece in _split(b, terms):
        d = jnp.dot(sel, piece, preferred_element_type=F32)
        out = d if out is None else out + d
    return out


def _silu(x):
    return x * jax.nn.sigmoid(x)


def _head_norm_lanes(o, width):
    parts = []
    for h in range(o.shape[1] // width):
        oh = o[:, h * width:(h + 1) * width]
        mu = jnp.mean(oh, axis=-1, keepdims=True)
        d = oh - mu
        var = jnp.mean(d * d, axis=-1, keepdims=True)
        parts.append(d * lax.rsqrt(var + NORM_EPS))
    return jnp.concatenate(parts, axis=1)


RET_L = 128


def _ret_tables():
    lg = np.log(1.0 - 2.0 ** (-5.0 - np.arange(4, dtype=np.float64)))
    i = np.arange(RET_L, dtype=np.float64)
    lane = np.arange(256)
    head_of_lane = (lane % 128) // 32
    ms = np.zeros((4 * RET_L, 256), np.float32)
    mq = np.zeros((4 * RET_L, 256), np.float32)
    dk = np.zeros((4 * RET_L, 256), np.float32)
    di = np.zeros((4 * RET_L, RET_L), np.float32)
    for h in range(4):
        m = (head_of_lane == h).astype(np.float64)
        ms[h * RET_L:(h + 1) * RET_L] = m[None, :]
        mq[h * RET_L:(h + 1) * RET_L] = np.exp((i + 1.0) * lg[h])[:, None] * m[None, :]
        dk[h * RET_L:(h + 1) * RET_L] = np.exp((RET_L - 1.0 - i) * lg[h])[:, None]
        rel = i[:, None] - i[None, :]
        di[h * RET_L:(h + 1) * RET_L] = np.where(rel >= 0, np.exp(np.maximum(rel, 0.0) * lg[h]), 0.0)
    dchunk = np.exp(RET_L * lg).astype(np.float32)
    return ms, mq, dk, di, dchunk


def _ret_kernel(x_ref, w_ref, cos_ref, sin_ref, ms_ref, mq_ref, dk_ref, di_ref, g_ref, y_ref, s_ref, *, dchunk):
    @pl.when(pl.program_id(1) == 0)
    def _():
        s_ref[...] = jnp.zeros_like(s_ref)

    p = jnp.dot(x_ref[0], w_ref[...], preferred_element_type=F32)
    L = RET_L
    state = [s_ref[:, h * 128:(h + 1) * 128] for h in range(4)]
    for c in range(TOKEN_BLOCK // L):
        sl = slice(c * L, (c + 1) * L)
        q, k = p[sl, 0:256], p[sl, 256:512]
        v, gate = p[sl, 512:1024], p[sl, 1024:1536]
        cs, sn = cos_ref[sl, :], sin_ref[sl, :]

        def rope(t):
            t1, t2 = t[:, :128], t[:, 128:]
            return jnp.concatenate([t1 * cs - t2 * sn, t1 * sn + t2 * cs], axis=1)

        qr = rope(q) * (64.0 ** -0.5)
        kr = rope(k)
        q4 = jnp.concatenate([qr] * 4, axis=0)
        scores = _dot_nt(q4 * ms_ref[...], kr) * di_ref[...]
        qd = q4 * mq_ref[...]
        k4 = jnp.concatenate([kr] * 4, axis=0) * dk_ref[...]
        hbs = [slice(h * L, (h + 1) * L) for h in range(4)]
        vhs = [v[:, h * 128:(h + 1) * 128] for h in range(4)]
        inter = [_dot(qd[hbs[h]], state[h]) for h in range(4)]
        upd = [_dot_tn(k4[hbs[h]], vhs[h]) for h in range(4)]
        intra = [_dot(scores[hbs[h]], vhs[h]) for h in range(4)]
        outs = [intra[h] + inter[h] for h in range(4)]
        state = [dchunk[h] * state[h] + upd[h] for h in range(4)]
        o = _head_norm_lanes(jnp.concatenate(outs, axis=1), 128)
        y_ref[0, sl, :] = (o * g_ref[...] * _silu(gate)).astype(y_ref.dtype)
    for h in range(4):
        s_ref[:, h * 128:(h + 1) * 128] = state[h]


def _retention(xb, w, cos, sin, g, tables):
    B, T, _ = xb.shape
    ms, mq, dk, di, dchunk = tables
    const = lambda shape: pl.BlockSpec(shape, lambda b, t: (0,) * len(shape))
    return pl.pallas_call(
        functools.partial(_ret_kernel, dchunk=tuple(float(d) for d in dchunk)),
        grid=(B, T // TOKEN_BLOCK),
        in_specs=[
            pl.BlockSpec((1, TOKEN_BLOCK, D_MODEL), lambda b, t: (b, t, 0)),
            const((D_MODEL, RET_W)),
            pl.BlockSpec((TOKEN_BLOCK, 128), lambda b, t: (t, 0)),
            pl.BlockSpec((TOKEN_BLOCK, 128), lambda b, t: (t, 0)),
            const((4 * RET_L, 256)), const((4 * RET_L, 256)), const((4 * RET_L, 256)), const((4 * RET_L, RET_L)),
            const((1, GROUP)),
        ],
        out_specs=pl.BlockSpec((1, TOKEN_BLOCK, GROUP), lambda b, t: (b, t, 0)),
        out_shape=jax.ShapeDtypeStruct((B, T, GROUP), BF16),
        scratch_shapes=[pltpu.VMEM((256, GROUP), F32)],
        compiler_params=pltpu.CompilerParams(dimension_semantics=("arbitrary", "arbitrary"),
                                             vmem_limit_bytes=VMEM_LIMIT),
        name="retention",
    )(xb, w, cos, sin, jnp.asarray(ms), jnp.asarray(mq), jnp.asarray(dk), jnp.asarray(di), g)


RWKV_L = 64


RWKV_HG = 2
RWKV_GW = RWKV_HG * 64


def _rwkv_tables(n_batch):
    lane = np.arange(RWKV_GW)
    row = np.arange(RWKV_GW)
    ms4 = (lane[None, :] // 64 == row[:, None] // 64).astype(np.float32)
    same = (row[:, None] // 64 == row[None, :] // 64)
    tt, ii = row[:, None] % 64, row[None, :] % 64
    strict = (same & (tt > ii)).astype(np.float32)
    incl = (same & (tt >= ii)).astype(np.float32)
    rt = np.arange(n_batch * TOKEN_BLOCK)
    tri = ((rt[:, None] // 64 == rt[None, :] // 64) & (rt[:, None] >= rt[None, :])).astype(np.float32)
    r5 = np.arange(512)
    bd = (r5[:, None] // 64 == r5[None, :] // 64).astype(np.float32)
    return ms4, strict, incl, tri, bd


def _rwkv_kernel(x_ref, w_ref, mu_ref, w0_ref, w2_ref, a0_ref, a2_ref, g2_ref, kk_ref, ka_ref, rk_ref, gn_ref,
                 ms4_ref, strict_ref, incl_ref, tri_ref, bd_ref, y_ref, p_scr, s_scr):
    NB, TC, L = x_ref.shape[0], TOKEN_BLOCK, RWKV_L
    GW, NG = RWKV_GW, GROUP // RWKV_GW

    @pl.when(pl.program_id(0) == 0)
    def _():
        s_scr[...] = jnp.zeros_like(s_scr)
        for b in range(NB):
            p_scr[b, pl.ds(0, 8), :] = jnp.zeros((8, RWKV_W), F32)

    p = jnp.dot(x_ref[...].reshape(NB * TC, D_MODEL), w_ref[...], preferred_element_type=F32)
    prevs = []
    for b in range(NB):
        pb = p[b * TC:(b + 1) * TC]
        p_scr[b, pl.ds(8, TC), :] = pb
        prevs.append(p_scr[b, pl.ds(7, TC), :])
        p_scr[b, pl.ds(0, 8), :] = pb[TC - 8:, :]
    prev = jnp.concatenate(prevs, axis=0)
    cols = p + (prev - p) * mu_ref[...]
    r, k, v = cols[:, 0:512], cols[:, 512:1024], cols[:, 1024:1536]
    tail = cols[:, 1536:1792]
    w = -jax.nn.softplus(-(w0_ref[...] + _dot(jnp.tanh(tail), w2_ref[...]))) - 0.5
    logdec = -jnp.exp(w)
    a = jax.nn.sigmoid(a0_ref[...] + _dot(tail, a2_ref[...]))
    g = _dot(jax.nn.sigmoid(tail), g2_ref[...])
    kk = k * kk_ref[...]
    bd = bd_ref[...]
    kk = kk / jnp.maximum(jnp.sqrt(_dot_sel(kk * kk, bd, 1)), 1e-12)
    k = k * (1.0 + (a - 1.0) * ka_ref[...])
    bonus = _dot_sel(r * k * rk_ref[...], bd, 1) * v
    cum = _sel_dot(tri_ref[...], logdec, 2)
    beta = kk * a
    ms4 = ms4_ref[...]
    strict, incl = strict_ref[...], incl_ref[...]

    def stack(t):
        return jnp.concatenate([t] * RWKV_HG, axis=0) * ms4

    chains = [(b, gi) for b in range(NB) for gi in range(NG)]
    CH = range(len(chains))
    st = [s_scr[b, gi] for b, gi in chains]
    outs = [[[None] * NG for _ in range(TC // L)] for _ in range(NB)]
    for c in range(TC // L):
        x1, x2, yy, z1, z2, sv, cl = ([None] * len(chains) for _ in range(7))
        for i, (b, gi) in enumerate(chains):
            rows = slice(b * TC + c * L, b * TC + (c + 1) * L)
            lanes = slice(gi * GW, (gi + 1) * GW)
            cm, lw = cum[rows, lanes], logdec[rows, lanes]
            cl[i] = cm[L - 1:L, :]
            gam, gam_ex, igam, gam_r = jnp.exp(cm), jnp.exp(cm - lw), jnp.exp(-cm), jnp.exp(cl[i] - cm)
            kk_c, r_c, k_c, v_c, b_c = kk[rows, lanes], r[rows, lanes], k[rows, lanes], v[rows, lanes], beta[rows, lanes]
            x1[i], x2[i] = stack(-kk_c * gam_ex), stack(r_c * gam)
            yy[i] = jnp.concatenate([stack(b_c * igam), stack(k_c * igam)], axis=0)
            z1[i], z2[i] = stack(b_c * gam_r), stack(k_c * gam_r)
            sv[i] = stack(v_c)
        ab = [_dot_nt(jnp.concatenate([x1[i], x2[i]], axis=0), yy[i]) for i in CH]
        a_b = [ab[i][0:GW, 0:GW] * strict for i in CH]
        a_k = [ab[i][0:GW, GW:2 * GW] * strict for i in CH]
        b_b = [ab[i][GW:2 * GW, 0:GW] * incl for i in CH]
        b_k = [ab[i][GW:2 * GW, GW:2 * GW] * incl for i in CH]
        u = [_dot(a_k[i], sv[i]) for i in CH]
        t0 = [_dot_nt(x1[i], st[i]) for i in CH]
        u = [u[i] + t0[i] for i in CH]
        pw = a_b
        for it in range(6):
            t = [_dot(pw[i], u[i]) for i in CH]
            if it < 5:
                pw = [_dot(pw[i], pw[i]) for i in CH]
            u = [u[i] + t[i] for i in CH]
        y0 = [_dot_nt(x2[i], st[i]) for i in CH]
        y1 = [_dot(b_b[i], u[i]) for i in CH]
        y2 = [_dot(b_k[i], sv[i]) for i in CH]
        s1 = [_dot_tn(u[i], z1[i]) for i in CH]
        s2 = [_dot_tn(sv[i], z2[i]) for i in CH]
        for i, (b, gi) in enumerate(chains):
            yst = y0[i] + y1[i] + y2[i]
            outs[b][c][gi] = sum(yst[h * L:(h + 1) * L] for h in range(RWKV_HG))
            st[i] = st[i] * jnp.exp(cl[i]) + s1[i] + s2[i]
    for i, (b, gi) in enumerate(chains):
        s_scr[b, gi] = st[i]

    y = jnp.concatenate([jnp.concatenate(o, axis=1) for ob in outs for o in ob], axis=0)
    mean = _dot_sel(y, bd, 1) * (1.0 / 64.0)
    d = y - mean
    var = _dot_sel(d * d, bd, 1) * (1.0 / 64.0)
    yn = d * lax.rsqrt(var + NORM_EPS) * gn_ref[...]
    y_ref[...] = ((yn + bonus) * g).astype(y_ref.dtype).reshape(NB, TC, GROUP)


def _rwkv(xb, w, mu, w0, w2p, a0, a2p, g2p, k_k, k_a, r_k, gn, tables):
    B, T, _ = xb.shape
    const = lambda shape: pl.BlockSpec(shape, lambda t: (0,) * len(shape))
    row = lambda n: const((1, n))
    return pl.pallas_call(
        _rwkv_kernel,
        grid=(T // TOKEN_BLOCK,),
        in_specs=[
            pl.BlockSpec((B, TOKEN_BLOCK, D_MODEL), lambda t: (0, t, 0)),
            const((D_MODEL, RWKV_W)), row(RWKV_W), row(GROUP), const((256, GROUP)), row(GROUP), const((256, GROUP)),
            const((256, GROUP)), row(GROUP), row(GROUP), row(GROUP), row(GROUP),
            const((RWKV_GW, RWKV_GW)), const((RWKV_GW, RWKV_GW)), const((RWKV_GW, RWKV_GW)),
            const((B * TOKEN_BLOCK, B * TOKEN_BLOCK)), const((GROUP, GROUP)),
        ],
        out_specs=pl.BlockSpec((B, TOKEN_BLOCK, GROUP), lambda t: (0, t, 0)),
        out_shape=jax.ShapeDtypeStruct((B, T, GROUP), BF16),
        scratch_shapes=[pltpu.VMEM((B, TOKEN_BLOCK + 8, RWKV_W), F32),
                        pltpu.VMEM((B, GROUP // RWKV_GW, RWKV_GW, RWKV_GW), F32)],
        compiler_params=pltpu.CompilerParams(dimension_semantics=("arbitrary",), vmem_limit_bytes=VMEM_LIMIT),
        name="rwkv7",
    )(xb, w, mu, w0, w2p, a0, a2p, g2p, k_k, k_a, r_k, gn,
      *[jnp.asarray(t, dt) for t, dt in zip(tables(B), (F32, F32, F32, BF16, BF16))])


GLA_SUB = 16


def _gla_tables():
    r = np.arange(256)
    tri16 = ((r[:, None] // 16 == r[None, :] // 16) & (r[:, None] >= r[None, :])).astype(np.float32)
    sel = (r[:, None] // 64 == np.arange(512)[None, :] // 128).astype(np.float32)
    ms = (np.arange(256)[None, :] // 64 == np.arange(64)[:, None] // 16).astype(np.float32)
    return tri16, sel, ms


def _gla_kernel(x_ref, w_ref, g2_ref, gb_ref, gn_ref, tri_ref, sel_ref, ms_ref, y_ref, st_scr):
    TC, SUB = TOKEN_BLOCK, GLA_SUB

    @pl.when(pl.program_id(1) == 0)
    def _():
        st_scr[...] = jnp.zeros_like(st_scr)

    p = jnp.dot(x_ref[0], w_ref[...], preferred_element_type=F32)
    gate = p[:, 1024:1536]
    la = jax.nn.log_sigmoid(_dot(p[:, 1536:1664], g2_ref[...]) + gb_ref[...]) * (1.0 / GLA_TAU)
    q, k, v = p[:, 0:256] * (64.0 ** -0.5), p[:, 256:512], p[:, 512:1024]
    b = _sel_dot(tri_ref[...], la, 3)
    sel = sel_ref[...]
    ms = ms_ref[...]
    row = lax.broadcasted_iota(jnp.int32, (SUB, 1), 0)
    outs = []
    st = st_scr[...]
    for s in range(TC // SUB):
        rows = slice(s * SUB, (s + 1) * SUB)
        qb, kb, vb, bb = q[rows], k[rows], v[rows], b[rows]
        bl = bb[SUB - 1:SUB, :]
        qe = jnp.concatenate([qb * jnp.exp(bb)] * 4, axis=0) * ms
        o_inter = jnp.concatenate(
            [_dot_nt(qe[h * SUB:(h + 1) * SUB], st[h * 128:(h + 1) * 128, :]) for h in range(4)], axis=1)
        ts = []
        for j in range(SUB):
            diff = jnp.where(row >= j, bb - bb[j:j + 1, :], -1e30)
            ts.append(qb * kb[j:j + 1, :] * jnp.exp(diff))
        rr = jnp.dot(jnp.concatenate(ts, axis=0).astype(BF16), sel, preferred_element_type=F32)
        o_intra = rr[0:SUB] * vb[0:1, :]
        for j in range(1, SUB):
            o_intra = o_intra + rr[j * SUB:(j + 1) * SUB] * vb[j:j + 1, :]
        outs.append(o_intra + o_inter)
        st = st * jnp.exp(bl) + _dot_tn(vb, kb * jnp.exp(bl - bb))
    st_scr[...] = st
    o = _head_norm_lanes(jnp.concatenate(outs, axis=0), 128)
    y_ref[0] = (o * gn_ref[...] * _silu(gate)).astype(y_ref.dtype)


def _gla(xb, w, g2p, gb, gn, tables):
    B, T, _ = xb.shape
    const = lambda shape: pl.BlockSpec(shape, lambda b, t: (0,) * len(shape))
    return pl.pallas_call(
        _gla_kernel,
        grid=(B, T // TOKEN_BLOCK),
        in_specs=[
            pl.BlockSpec((1, TOKEN_BLOCK, D_MODEL), lambda b, t: (b, t, 0)),
            const((D_MODEL, GLA_W)), const((128, 256)), const((1, 256)), const((1, GROUP)),
            const((256, 256)), const((256, 512)), const((64, 256)),
        ],
        out_specs=pl.BlockSpec((1, TOKEN_BLOCK, GROUP), lambda b, t: (b, t, 0)),
        out_shape=jax.ShapeDtypeStruct((B, T, GROUP), BF16),
        scratch_shapes=[pltpu.VMEM((GROUP, 256), F32)],
        compiler_params=pltpu.CompilerParams(dimension_semantics=("arbitrary", "arbitrary"),
                                             vmem_limit_bytes=VMEM_LIMIT),
        name="gla",
    )(xb, w, g2p, gb, gn, *[jnp.asarray(t, dt) for t, dt in zip(tables, (BF16, BF16, F32))])


MLSTM_L = 128


def _mlstm_tables():
    r = np.arange(TOKEN_BLOCK)
    tri = ((r[:, None] // MLSTM_L == r[None, :] // MLSTM_L) & (r[:, None] >= r[None, :])).astype(np.float32)
    ms = (np.arange(256)[None, :] // 64 == np.arange(4 * MLSTM_L)[:, None] // MLSTM_L).astype(np.float32)
    return tri, ms


def _mlstm_kernel(x_ref, w_ref, conv_ref, gbias_ref, gn_ref, tri_ref, ms_ref, y_ref,
                  qk_scr, c_scr, n_scr, m_scr):
    TC, L = TOKEN_BLOCK, MLSTM_L

    @pl.when(pl.program_id(1) == 0)
    def _():
        c_scr[...] = jnp.zeros_like(c_scr)
        n_scr[...] = jnp.zeros_like(n_scr)
        m_scr[...] = jnp.zeros_like(m_scr)
        qk_scr[pl.ds(0, 8), :] = jnp.zeros((8, 512), F32)

    p = jnp.dot(x_ref[0], w_ref[...], preferred_element_type=F32)
    qk_scr[pl.ds(8, TC), :] = p[:, 0:512]
    conv = p[:, 0:512] * conv_ref[3:4, :]
    for j in range(3):
        conv = conv + qk_scr[pl.ds(5 + j, TC), :] * conv_ref[j:j + 1, :]
    qk_scr[pl.ds(0, 8), :] = p[TC - 8:, 0:512]
    qk = _silu(conv)
    q, k = qk[:, 0:256], qk[:, 256:512] * (64.0 ** -0.5)
    v, og = p[:, 512:1024], p[:, 1024:1536]
    tail = p[:, 1536:1664] + gbias_ref[...]
    lane = lax.broadcasted_iota(jnp.int32, (1, 128), 1)
    logf = jnp.where((lane >= 4) & (lane < 8), jax.nn.log_sigmoid(tail), 0.0)
    cumf = _sel_dot(tri_ref[...], logf, 3)
    gcol = jnp.where(lane < 4, tail, cumf)
    eye = (lax.broadcasted_iota(jnp.int32, (128, 128), 0) == lax.broadcasted_iota(jnp.int32, (128, 128), 1)).astype(BF16)
    ms = ms_ref[...]
    ri = lax.broadcasted_iota(jnp.int32, (L, L), 0)
    ci = lax.broadcasted_iota(jnp.int32, (L, L), 1)
    causal = ri >= ci
    lane256 = lax.broadcasted_iota(jnp.int32, (1, 256), 1)
    c_state = [c_scr[:, h * 128:(h + 1) * 128] for h in range(4)]
    m_state = [m_scr[0:1, h:h + 1] for h in range(4)]
    n_state = n_scr[...]

    for c in range(TC // L):
        rows = slice(c * L, (c + 1) * L)
        gc = gcol[rows, :]
        gr = sum(lax.dot_general(eye, piece, (((1,), (1,)), ((), ())), preferred_element_type=F32)
                 for piece in _split(gc, 3))
        qc, kc, vc = q[rows], k[rows], v[rows]
        q4 = jnp.concatenate([qc] * 4, axis=0) * ms
        scores = _dot_nt(q4, kc)
        qn = jnp.sum(q4 * n_state, axis=1, keepdims=True)
        hbs = [slice(h * L, (h + 1) * L) for h in range(4)]
        vhs = [vc[:, h * 128:(h + 1) * 128] for h in range(4)]
        qc_state = [_dot(q4[hbs[h]], c_state[h]) for h in range(4)]
        ss, inters, m_rows, wj_cols, wrows, scales = [], [], [], [], [], []
        n_scale = jnp.zeros((1, 256), F32)
        for h in range(4):
            b_col, i_col = gc[:, 4 + h:5 + h], gc[:, h:h + 1]
            b_row, i_row = gr[4 + h:5 + h, :], gr[h:h + 1, :]
            m_prev = m_state[h]
            log_d = jnp.where(causal, b_col - b_row + i_row, -jnp.inf)
            m_inter = b_col + m_prev
            m_row = jnp.maximum(jnp.max(log_d, axis=-1, keepdims=True), m_inter)
            ss.append(scores[hbs[h]] * jnp.exp(log_d - m_row))
            inters.append(jnp.exp(m_inter - m_row))
            m_rows.append(m_row)
            m_new = m_row[L - 1:L, :]
            b_last = b_col[L - 1:L, :]
            wj_cols.append(jnp.exp(b_last - b_col + i_col - m_new))
            wrows.append(jnp.exp(b_last - b_row + i_row - m_new))
            scale = jnp.exp(b_last + m_prev - m_new)
            scales.append(scale)
            n_scale = jnp.where(lane256 // 64 == h, scale, n_scale)
            m_state[h] = m_new
        sv = [_dot(ss[h], vhs[h]) for h in range(4)]
        upd = [_dot_tn(kc * wj_cols[h], vhs[h]) for h in range(4)]
        outs = []
        for h in range(4):
            num = sv[h] + inters[h] * qc_state[h]
            den = jnp.sum(ss[h], axis=-1, keepdims=True) + inters[h] * qn[hbs[h]]
            outs.append(num / jnp.maximum(jnp.abs(den), jnp.exp(-m_rows[h])))
            c_state[h] = scales[h] * c_state[h] + upd[h]
        wk = _dot(jnp.concatenate(wrows + [jnp.zeros((4, L), F32)], axis=0), kc)
        n_add = jnp.zeros((1, 256), F32)
        for h in range(4):
            n_add = jnp.where(lane256 // 64 == h, wk[h:h + 1, :], n_add)
        n_state = n_scale * n_state + n_add
        o = _head_norm_lanes(jnp.concatenate(outs, axis=1), 128)
        y_ref[0, rows, :] = (o * gn_ref[...] * jax.nn.sigmoid(og[rows])).astype(y_ref.dtype)

    n_scr[...] = n_state
    for h in range(4):
        c_scr[:, h * 128:(h + 1) * 128] = c_state[h]
        m_scr[0:1, h:h + 1] = m_state[h]


def _mlstm(xb, w, conv_w, gbias, gn, tables):
    B, T, _ = xb.shape
    const = lambda shape: pl.BlockSpec(shape, lambda b, t: (0,) * len(shape))
    return pl.pallas_call(
        _mlstm_kernel,
        grid=(B, T // TOKEN_BLOCK),
        in_specs=[
            pl.BlockSpec((1, TOKEN_BLOCK, D_MODEL), lambda b, t: (b, t, 0)),
            const((D_MODEL, MLSTM_W)), const((4, 512)), const((1, 128)), const((1, GROUP)),
            const((TOKEN_BLOCK, TOKEN_BLOCK)), const((4 * MLSTM_L, 256)),
        ],
        out_specs=pl.BlockSpec((1, TOKEN_BLOCK, GROUP), lambda b, t: (b, t, 0)),
        out_shape=jax.ShapeDtypeStruct((B, T, GROUP), BF16),
        scratch_shapes=[pltpu.VMEM((TOKEN_BLOCK + 8, 512), F32), pltpu.VMEM((256, GROUP), F32),
                        pltpu.VMEM((1, 256), F32), pltpu.VMEM((8, 128), F32)],
        compiler_params=pltpu.CompilerParams(dimension_semantics=("arbitrary", "arbitrary"),
                                             vmem_limit_bytes=VMEM_LIMIT),
        name="mlstm",
    )(xb, w, conv_w, gbias, gn, *[jnp.asarray(t, dt) for t, dt in zip(tables, (BF16, F32))])


ROW_BLOCK = 512


def _layer_norm(z, g, b):
    mu = jnp.mean(z, axis=-1, keepdims=True)
    d = z - mu
    var = jnp.mean(d * d, axis=-1, keepdims=True)
    return d * lax.rsqrt(var + NORM_EPS) * g + b


def _outproj_kernel(y0_ref, y1_ref, y2_ref, y3_ref, x_ref, w_ref, g_ref, b_ref, rwh_ref, rwl_ref, rb_ref,
                    x1_ref, x1b_ref, idx_ref, gate_ref):
    acc = jnp.dot(y0_ref[...], w_ref[0:512, :], preferred_element_type=F32)
    acc += jnp.dot(y1_ref[...], w_ref[512:1024, :], preferred_element_type=F32)
    acc += jnp.dot(y2_ref[...], w_ref[1024:1536, :], preferred_element_type=F32)
    acc += jnp.dot(y3_ref[...], w_ref[1536:2048, :], preferred_element_type=F32)
    x1 = _layer_norm(DN_ALPHA * x_ref[...] + acc, g_ref[...], b_ref[...])
    x1_ref[...] = x1
    x1b_ref[...] = x1.astype(BF16)
    x_hi, x_lo = _split(x1, 2)
    logits = (jnp.dot(x_hi, rwh_ref[...], preferred_element_type=F32)
              + jnp.dot(x_lo, rwh_ref[...], preferred_element_type=F32)
              + jnp.dot(x_hi, rwl_ref[...], preferred_element_type=F32)) + rb_ref[...]
    lane = lax.broadcasted_iota(jnp.int32, logits.shape, 1)
    logits = jnp.where(lane < N_EXPERTS, logits, -jnp.inf)
    idx_out = jnp.zeros(logits.shape, jnp.int32)
    gate_out = jnp.zeros(logits.shape, F32)
    top = None
    for kth in range(TOP_K):
        m = jnp.max(logits, axis=-1, keepdims=True)
        sel = jnp.min(jnp.where(logits == m, lane, LANE), axis=-1, keepdims=True)
        if kth == 0:
            top = m
        idx_out = jnp.where(lane == kth, sel, idx_out)
        gate_out = jnp.where(lane == kth, jnp.exp(m - top), gate_out)
        logits = jnp.where(lane == sel, -jnp.inf, logits)
    gate_out = gate_out / jnp.sum(gate_out, axis=-1, keepdims=True)
    idx_ref[...] = idx_out
    gate_ref[...] = gate_out


def _outproj(ys, x, w_out, g, b, rw, rb):
    N = x.shape[0]
    rb_ = ROW_BLOCK
    const = lambda shape: pl.BlockSpec(shape, lambda i: (0,) * len(shape))
    rows = lambda width: pl.BlockSpec((rb_, width), lambda i: (i, 0))
    rw_hi = rw.astype(BF16)
    rw_lo = (rw - rw_hi.astype(F32)).astype(BF16)
    return pl.pallas_call(
        _outproj_kernel,
        grid=(N // rb_,),
        in_specs=[rows(GROUP)] * 4 + [rows(D_MODEL), const((D_MODEL, D_MODEL)), const((1, D_MODEL)), const((1, D_MODEL)),
                                      const((D_MODEL, LANE)), const((D_MODEL, LANE)), const((1, LANE))],
        out_specs=[rows(D_MODEL), rows(D_MODEL), rows(LANE), rows(LANE)],
        out_shape=[jax.ShapeDtypeStruct((N, D_MODEL), F32), jax.ShapeDtypeStruct((N, D_MODEL), BF16),
                   jax.ShapeDtypeStruct((N, LANE), jnp.int32), jax.ShapeDtypeStruct((N, LANE), F32)],
        compiler_params=pltpu.CompilerParams(dimension_semantics=("arbitrary",), vmem_limit_bytes=VMEM_LIMIT),
        name="outproj_norm_router",
    )(*ys, x, w_out, g, b, rw_hi, rw_lo, rb)


MOE_BLOCK = 512
FF_TILE = 1024
DN_TILE = 1024
MOE_SPLIT = 3


def _count_le(bounds, q):
    return jnp.sum((q[:, None] >= bounds[None, :]).astype(jnp.int32), axis=1)


def _ffn_schedule(pcounts, pstarts, lo, hi, n_tiles):
    b0 = jnp.clip(pstarts // MOE_BLOCK, lo, hi).astype(jnp.int32)
    b1 = jnp.clip((pstarts + pcounts) // MOE_BLOCK, lo, hi).astype(jnp.int32)
    nblk, bstart = b1 - b0, b0 - lo
    steps_e = nblk * n_tiles
    ends = jnp.cumsum(steps_e).astype(jnp.int32)
    begins = ends - steps_e
    total = ends[-1]
    s = jnp.arange((hi - lo) * n_tiles, dtype=jnp.int32)
    e = jnp.minimum(_count_le(ends, s), N_EXPERTS - 1)
    r = s - begins[e]
    nb_e = jnp.maximum(nblk[e], 1)
    til, j = r // nb_e, r % nb_e
    blk = bstart[e] + j
    last = jnp.maximum(total - 1, 0)
    valid = s < total
    rp = s - total
    blk = jnp.where(valid, blk, total // n_tiles + rp // n_tiles)
    otil = jnp.where(valid, til, rp % n_tiles)
    wtil = jnp.where(valid, til, til[last])
    e = jnp.where(valid, e, e[last])
    first = (valid & (j == 0)).astype(jnp.int32)
    i32 = lambda a: a.astype(jnp.int32)
    return i32(blk), i32(otil), i32(wtil), i32(e), first, i32(total).reshape(1)


def _ffn_up_kernel(blk_ref, otil_ref, wtil_ref, ex_ref, first_ref, ns_ref,
                   xs_ref, wg_ref, wl_ref, bg_ref, bl_ref, act_ref, wg_bf, wl_bf):
    s = pl.program_id(0)

    @pl.when(first_ref[s] == 1)
    def _():
        wg_bf[...] = wg_ref[...].astype(BF16)
        wl_bf[...] = wl_ref[...].astype(BF16)

    @pl.when(s < ns_ref[0])
    def _():
        xs = xs_ref[...]
        glu = jnp.dot(xs, wg_bf[...], preferred_element_type=F32) + bg_ref[...]
        lin = jnp.dot(xs, wl_bf[...], preferred_element_type=F32) + bl_ref[...]
        glu = jnp.minimum(glu, SWIGLU_LIMIT)
        lin = jnp.clip(lin, -SWIGLU_LIMIT, SWIGLU_LIMIT)
        act_ref[...] = (glu * jax.nn.sigmoid(SWIGLU_ALPHA * glu) * (lin + 1.0)).astype(act_ref.dtype)

    @pl.when(s >= ns_ref[0])
    def _():
        act_ref[...] = jnp.zeros_like(act_ref)


def _ffn_dn_kernel(blk_ref, otil_ref, wtil_ref, ex_ref, first_ref, ns_ref, *refs, ranges):
    act_refs = refs[:len(ranges)]
    wd_ref, bd_ref, o_ref, wd_bf = refs[len(ranges):]
    s = pl.program_id(0)
    blk = blk_ref[s]

    @pl.when(first_ref[s] == 1)
    def _():
        wd_bf[...] = wd_ref[...].astype(BF16)

    for act_ref, (lo, hi) in zip(act_refs, ranges):
        @pl.when((s < ns_ref[0]) & (blk >= lo) & (blk < hi))
        def _():
            out = jnp.dot(act_ref[...], wd_bf[...], preferred_element_type=F32) + bd_ref[...]
            o_ref[...] = out.astype(o_ref.dtype)

    @pl.when(s >= ns_ref[0])
    def _():
        o_ref[...] = jnp.zeros_like(o_ref)


def _sched_maps(layer):
    wtile = lambda off: (lambda s, blk, otil, wtil, ex, first, ns: (layer, ex[s], 0, off + wtil[s]))
    rows_full = lambda s, blk, otil, wtil, ex, first, ns: (blk[s], 0)
    rows_tile = lambda s, blk, otil, wtil, ex, first, ns: (blk[s], otil[s])
    return wtile, rows_full, rows_tile


def _expert_up(layer, xs, sched, w_gu, b_gu):
    n_rows = xs.shape[0]
    nt = D_FF // FF_TILE
    wtile, rows_full, rows_tile = _sched_maps(layer)
    return pl.pallas_call(
        _ffn_up_kernel,
        grid_spec=pltpu.PrefetchScalarGridSpec(
            num_scalar_prefetch=6,
            grid=((n_rows // MOE_BLOCK) * nt,),
            in_specs=[
                pl.BlockSpec((MOE_BLOCK, D_MODEL), rows_full),
                pl.BlockSpec((None, None, D_MODEL, FF_TILE), wtile(0)),
                pl.BlockSpec((None, None, D_MODEL, FF_TILE), wtile(nt)),
                pl.BlockSpec((None, None, 1, FF_TILE), wtile(0)),
                pl.BlockSpec((None, None, 1, FF_TILE), wtile(nt)),
            ],
            out_specs=pl.BlockSpec((MOE_BLOCK, FF_TILE), rows_tile),
            scratch_shapes=[pltpu.VMEM((D_MODEL, FF_TILE), BF16), pltpu.VMEM((D_MODEL, FF_TILE), BF16)],
        ),
        out_shape=jax.ShapeDtypeStruct((n_rows, D_FF), BF16),
        compiler_params=pltpu.CompilerParams(dimension_semantics=("arbitrary",), vmem_limit_bytes=VMEM_LIMIT),
        name="expert_up",
    )(*sched, xs, w_gu, w_gu, b_gu, b_gu)


def _expert_down(layer, acts, ranges, sched, w_dn, b_dn):
    n_blocks = ranges[-1][1]
    wtile, rows_full, rows_tile = _sched_maps(layer)
    part = lambda lo, hi: (lambda s, blk, otil, wtil, ex, first, ns: (jnp.clip(blk[s] - lo, 0, hi - lo - 1), 0))
    return pl.pallas_call(
        functools.partial(_ffn_dn_kernel, ranges=tuple(ranges)),
        grid_spec=pltpu.PrefetchScalarGridSpec(
            num_scalar_prefetch=6,
            grid=(n_blocks * (D_MODEL // DN_TILE),),
            in_specs=[pl.BlockSpec((MOE_BLOCK, D_FF), part(lo, hi)) for lo, hi in ranges] + [
                pl.BlockSpec((None, None, D_FF, DN_TILE), wtile(0)),
                pl.BlockSpec((None, None, 1, DN_TILE), wtile(0)),
            ],
            out_specs=pl.BlockSpec((MOE_BLOCK, DN_TILE), rows_tile),
            scratch_shapes=[pltpu.VMEM((D_FF, DN_TILE), BF16)],
        ),
        out_shape=jax.ShapeDtypeStruct((n_blocks * MOE_BLOCK, D_MODEL), BF16),
        compiler_params=pltpu.CompilerParams(dimension_semantics=("arbitrary",), vmem_limit_bytes=VMEM_LIMIT),
        name="expert_down",
    )(*sched, *acts, w_dn, b_dn)


def _combine_kernel(x_ref, y0_ref, y1_ref, y2_ref, y3_ref, gate_ref, g_ref, b_ref, o_ref, ob_ref):
    gate = gate_ref[...]
    y = ((y0_ref[...].astype(F32) * gate[:, 0:1] + y1_ref[...].astype(F32) * gate[:, 1:2])
         + (y2_ref[...].astype(F32) * gate[:, 2:3] + y3_ref[...].astype(F32) * gate[:, 3:4]))
    x2 = _layer_norm(DN_ALPHA * x_ref[...] + y, g_ref[...], b_ref[...])
    o_ref[...] = x2
    ob_ref[...] = x2.astype(BF16)


def _combine(x1, ysg, gate, g, b):
    N = x1.shape[0]
    rb_ = ROW_BLOCK
    const = lambda shape: pl.BlockSpec(shape, lambda i: (0,) * len(shape))
    rows = pl.BlockSpec((rb_, D_MODEL), lambda i: (i, 0))
    return pl.pallas_call(
        _combine_kernel,
        grid=(N // rb_,),
        in_specs=[rows] * (1 + TOP_K) + [pl.BlockSpec((rb_, LANE), lambda i: (i, 0)),
                                         const((1, D_MODEL)), const((1, D_MODEL))],
        out_specs=[rows] * 2,
        out_shape=[jax.ShapeDtypeStruct((N, D_MODEL), F32), jax.ShapeDtypeStruct((N, D_MODEL), BF16)],
        compiler_params=pltpu.CompilerParams(dimension_semantics=("arbitrary",), vmem_limit_bytes=VMEM_LIMIT),
        name="combine_norm",
    )(x1, *ysg, gate, g, b)


def _route(top_idx):
    n_tok = top_idx.shape[0]
    n_assign = n_tok * TOP_K
    flat_e = top_idx.reshape(-1).astype(jnp.int32)
    iota = jnp.arange(n_assign, dtype=jnp.int32)
    experts = jnp.arange(N_EXPERTS, dtype=jnp.int32)
    counts = jnp.sum((flat_e[:, None] == experts[None, :]).astype(jnp.int32), axis=0)
    starts = jnp.cumsum(counts) - counts
    pcounts = (counts + MOE_BLOCK - 1) // MOE_BLOCK * MOE_BLOCK
    pends = jnp.cumsum(pcounts)
    pstarts = pends - pcounts
    _, order = lax.sort((flat_e, iota), num_keys=1)
    shift = jnp.sum(jnp.where(iota[:, None] >= (starts + counts)[None, :], (pcounts - counts)[None, :], 0), axis=1)
    dest = (iota + shift).astype(jnp.int32)
    n_slots = -(-n_assign // MOE_BLOCK) * MOE_BLOCK + N_EXPERTS * MOE_BLOCK
    n_blocks = n_slots // MOE_BLOCK
    block_start = jnp.arange(n_blocks, dtype=jnp.int32) * MOE_BLOCK
    block_exp = jnp.minimum(_count_le(pends, block_start), N_EXPERTS - 1)
    boff = block_start - pstarts[block_exp]
    first_a = starts[block_exp] + boff
    n_valid = jnp.clip(counts[block_exp] - boff, 0, MOE_BLOCK)
    j = jnp.arange(MOE_BLOCK, dtype=jnp.int32)
    valid = j[None, :] < n_valid[:, None]
    src = order[jnp.clip(first_a[:, None] + j[None, :], 0, n_assign - 1)]
    filler = (block_start[:, None] + j[None, :]) % n_tok
    slot_tok = jnp.where(valid, src // TOP_K, filler).astype(jnp.int32).reshape(-1)
    _, pos = lax.sort((order, dest), num_keys=1)
    ranges = [(p * n_blocks // MOE_SPLIT, (p + 1) * n_blocks // MOE_SPLIT) for p in range(MOE_SPLIT)]
    sched_up = [_ffn_schedule(pcounts, pstarts, lo, hi, D_FF // FF_TILE) for lo, hi in ranges]
    sched_dn = _ffn_schedule(pcounts, pstarts, 0, n_blocks, D_MODEL // DN_TILE)
    return slot_tok, pos, ranges, sched_up, sched_dn


def _pad_rows(w, row0, total):
    return jnp.zeros((total, w.shape[1]), F32).at[row0:row0 + w.shape[0]].set(w)


def kernel(x, w_in, rwkv_mu, rwkv_w0, rwkv_w2, rwkv_a0, rwkv_a2, rwkv_g2, rwkv_k_k, rwkv_k_a, rwkv_r_k, gla_g2, gla_gb, mlstm_conv, mlstm_ib, mlstm_fb, head_norm_g, w_out, ln1_g, ln1_b, router_w, router_b, w_gu, b_gu, w_dn, b_dn, ln2_g, ln2_b):
    B, T, D = x.shape
    N = B * T
    depth = w_in.shape[0]
    ret_tab, rwkv_tab, gla_tab, mlstm_tab = _ret_tables(), _rwkv_tables, _gla_tables(), _mlstm_tables()
    pos_t = jnp.arange(T, dtype=F32)
    inv = ROPE_BASE ** (-jnp.arange(32, dtype=F32) / 32)
    ang = pos_t[:, None] * inv[None, :]
    cos = jnp.tile(jnp.cos(ang), (1, 4))
    sin = jnp.tile(jnp.sin(ang), (1, 4))
    row = lambda v: v.reshape(1, -1)

    b_gu4 = b_gu.reshape(depth, N_EXPERTS, 1, 2 * D_FF)
    b_dn4 = b_dn.reshape(depth, N_EXPERTS, 1, D_MODEL)
    take_rows = lambda a, i: a.at[i].get(mode='promise_in_bounds')

    xf = x.reshape(N, D)
    xb = x.astype(BF16)
    for l in range(depth):
        w_ext = jnp.concatenate([w_in[l], jnp.zeros((D, 1), F32)], axis=1)
        seg = lambda cols: jnp.take(w_ext, jnp.asarray(cols), axis=1).astype(BF16)
        gn = head_norm_g[l]
        y_ret = _retention(xb, seg(_ret_cols()), cos, sin, row(gn[0:512]), ret_tab)
        mu = jnp.concatenate([rwkv_mu[l], jnp.zeros((96,), F32)])
        y_rwkv = _rwkv(xb, seg(_rwkv_cols()), row(mu), row(rwkv_w0[l]), _pad_rows(rwkv_w2[l], 0, 256),
                       row(rwkv_a0[l]), _pad_rows(rwkv_a2[l], 32, 256), _pad_rows(rwkv_g2[l], 64, 256),
                       row(rwkv_k_k[l]), row(rwkv_k_a[l]), row(rwkv_r_k[l]), row(gn[512:1024]), rwkv_tab)
        y_gla = _gla(xb, seg(_gla_cols()), _pad_rows(gla_g2[l], 0, 128), row(gla_gb[l]), row(gn[1024:1536]), gla_tab)
        gbias = jnp.concatenate([mlstm_ib[l], mlstm_fb[l], jnp.zeros((120,), F32)])
        y_mlstm = _mlstm(xb, seg(_mlstm_cols()), mlstm_conv[l], row(gbias), row(gn[1536:2048]), mlstm_tab)
        ys = [y.reshape(N, GROUP) for y in (y_ret, y_rwkv, y_gla, y_mlstm)]
        rw = jnp.zeros((D, LANE), F32).at[:, :N_EXPERTS].set(router_w[l])
        rb = jnp.zeros((1, LANE), F32).at[0, :N_EXPERTS].set(router_b[l])
        x1, x1b, idx, gate = _outproj(ys, xf, w_out[l].astype(BF16), row(ln1_g[l]), row(ln1_b[l]), rw, rb)
        slot_tok, pos, ranges, sched_up, sched_dn = _route(idx[:, :TOP_K])
        acts = [_expert_up(l, take_rows(x1b, slot_tok[lo * MOE_BLOCK:hi * MOE_BLOCK]), sched, w_gu, b_gu4)
                for (lo, hi), sched in zip(ranges, sched_up)]
        ys_moe = _expert_down(l, acts, ranges, sched_dn, w_dn, b_dn4)
        pos = pos.reshape(N, TOP_K)
        ysg = [take_rows(ys_moe, pos[:, kth]) for kth in range(TOP_K)]
        xf, xb2 = _combine(x1, ysg, gate, row(ln2_g[l]), row(ln2_b[l]))
        xb = xb2.reshape(B, T, D)
    return xf.reshape(B, T, D)
```

```python
import functools
import math

import numpy as np
import jax
import jax.numpy as jnp
from jax import lax
from jax.experimental import pallas as pl
from jax.experimental.pallas import tpu as pltpu

F32 = jnp.float32
BF16 = jnp.bfloat16
HI = lax.Precision.HIGHEST

D_MODEL = 2048
GROUP = 512
N_EXPERTS = 32
TOP_K = 4
D_FF = 2048
DEPTH = 2
NORM_EPS = 1e-5
DN_ALPHA = (2 * DEPTH) ** 0.25
ROPE_BASE = 10000.0
GLA_TAU = 16.0
SWIGLU_ALPHA = 1.702
SWIGLU_LIMIT = 7.0

LANE = 128
TOKEN_BLOCK = 256
VMEM_LIMIT = 56 * 1024 * 1024

RET_OFF, RWKV_OFF, GLA_OFF, MLSTM_OFF, D_IN = 0, 1536, 3232, 4784, 6328
RET_W, RWKV_W, GLA_W, MLSTM_W = 1536, 1792, 1664, 1664


def _ret_cols():
    def half_split(base):
        lo = [base + h * 64 + j for h in range(4) for j in range(32)]
        hi = [base + h * 64 + 32 + j for h in range(4) for j in range(32)]
        return lo + hi
    cols = half_split(RET_OFF) + half_split(RET_OFF + 256)
    cols += list(range(RET_OFF + 512, RET_OFF + 1536))
    return np.asarray(cols, np.int32)


def _rwkv_cols():
    cols = list(range(RWKV_OFF, RWKV_OFF + 1696)) + [D_IN] * 96
    return np.asarray(cols, np.int32)


def _gla_cols():
    o = GLA_OFF
    cols = list(range(o, o + 1024)) + list(range(o + 1040, o + 1552)) + list(range(o + 1024, o + 1040)) + [D_IN] * 112
    return np.asarray(cols, np.int32)


def _mlstm_cols():
    o = MLSTM_OFF
    cols = list(range(o, o + 1024)) + list(range(o + 1032, o + 1544)) + list(range(o + 1024, o + 1032)) + [D_IN] * 120
    return np.asarray(cols, np.int32)


def _dot(a, b):
    return jnp.dot(a.astype(BF16), b.astype(BF16), preferred_element_type=F32)


def _dot_nt(a, b):
    return lax.dot_general(a.astype(BF16), b.astype(BF16), (((1,), (1,)), ((), ())), preferred_element_type=F32)


def _dot_tn(a, b):
    return lax.dot_general(a.astype(BF16), b.astype(BF16), (((0,), (0,)), ((), ())), preferred_element_type=F32)


def _split(a, terms):
    pieces, rem = [], a
    for t in range(terms):
        piece = rem.astype(BF16)
        pieces.append(piece)
        if t + 1 < terms:
            rem = rem - piece.astype(F32)
    return pieces


def _dot_sel(a, sel, terms):
    out = None
    for piece in _split(a, terms):
        d = jnp.dot(piece, sel, preferred_element_type=F32)
        out = d if out is None else out + d
    return out


def _sel_dot(sel, b, terms):
    out = None
    for piece in _split(b, terms):
        d = jnp.dot(sel, piece, preferred_element_type=F32)
        out = d if out is None else out + d
    return out


def _silu(x):
    return x * jax.nn.sigmoid(x)


def _head_norm_lanes(o, width):
    parts = []
    for h in range(o.shape[1] // width):
        oh = o[:, h * width:(h + 1) * width]
        mu = jnp.mean(oh, axis=-1, keepdims=True)
        d = oh - mu
        var = jnp.mean(d * d, axis=-1, keepdims=True)
        parts.append(d * lax.rsqrt(var + NORM_EPS))
    return jnp.concatenate(parts, axis=1)


RET_L = 128


def _ret_tables():
    lg = np.log(1.0 - 2.0 ** (-5.0 - np.arange(4, dtype=np.float64)))
    i = np.arange(RET_L, dtype=np.float64)
    lane = np.arange(256)
    head_of_lane = (lane % 128) // 32
    ms = np.zeros((4 * RET_L, 256), np.float32)
    mq = np.zeros((4 * RET_L, 256), np.float32)
    dk = np.zeros((4 * RET_L, 256), np.float32)
    di = np.zeros((4 * RET_L, RET_L), np.float32)
    for h in range(4):
        m = (head_of_lane == h).astype(np.float64)
        ms[h * RET_L:(h + 1) * RET_L] = m[None, :]
        mq[h * RET_L:(h + 1) * RET_L] = np.exp((i + 1.0) * lg[h])[:, None] * m[None, :]
        dk[h * RET_L:(h + 1) * RET_L] = np.exp((RET_L - 1.0 - i) * lg[h])[:, None]
        rel = i[:, None] - i[None, :]
        di[h * RET_L:(h + 1) * RET_L] = np.where(rel >= 0, np.exp(np.maximum(rel, 0.0) * lg[h]), 0.0)
    dchunk = np.exp(RET_L * lg).astype(np.float32)
    return ms, mq, dk, di, dchunk


def _ret_kernel(x_ref, w_ref, cos_ref, sin_ref, ms_ref, mq_ref, dk_ref, di_ref, g_ref, y_ref, s_ref, *, dchunk):
    @pl.when(pl.program_id(1) == 0)
    def _():
        s_ref[...] = jnp.zeros_like(s_ref)

    p = jnp.dot(x_ref[0], w_ref[...], preferred_element_type=F32)
    L = RET_L
    state = [s_ref[:, h * 128:(h + 1) * 128] for h in range(4)]
    for c in range(TOKEN_BLOCK // L):
        sl = slice(c * L, (c + 1) * L)
        q, k = p[sl, 0:256], p[sl, 256:512]
        v, gate = p[sl, 512:1024], p[sl, 1024:1536]
        cs, sn = cos_ref[sl, :], sin_ref[sl, :]

        def rope(t):
            t1, t2 = t[:, :128], t[:, 128:]
            return jnp.concatenate([t1 * cs - t2 * sn, t1 * sn + t2 * cs], axis=1)

        qr = rope(q) * (64.0 ** -0.5)
        kr = rope(k)
        q4 = jnp.concatenate([qr] * 4, axis=0)
        scores = _dot_nt(q4 * ms_ref[...], kr) * di_ref[...]
        qd = q4 * mq_ref[...]
        k4 = jnp.concatenate([kr] * 4, axis=0) * dk_ref[...]
        hbs = [slice(h * L, (h + 1) * L) for h in range(4)]
        vhs = [v[:, h * 128:(h + 1) * 128] for h in range(4)]
        inter = [_dot(qd[hbs[h]], state[h]) for h in range(4)]
        upd = [_dot_tn(k4[hbs[h]], vhs[h]) for h in range(4)]
        intra = [_dot(scores[hbs[h]], vhs[h]) for h in range(4)]
        outs = [intra[h] + inter[h] for h in range(4)]
        state = [dchunk[h] * state[h] + upd[h] for h in range(4)]
        o = _head_norm_lanes(jnp.concatenate(outs, axis=1), 128)
        y_ref[0, sl, :] = (o * g_ref[...] * _silu(gate)).astype(y_ref.dtype)
    for h in range(4):
        s_ref[:, h * 128:(h + 1) * 128] = state[h]


def _retention(xb, w, cos, sin, g, tables):
    B, T, _ = xb.shape
    ms, mq, dk, di, dchunk = tables
    const = lambda shape: pl.BlockSpec(shape, lambda b, t: (0,) * len(shape))
    return pl.pallas_call(
        functools.partial(_ret_kernel, dchunk=tuple(float(d) for d in dchunk)),
        grid=(B, T // TOKEN_BLOCK),
        in_specs=[
            pl.BlockSpec((1, TOKEN_BLOCK, D_MODEL), lambda b, t: (b, t, 0)),
            const((D_MODEL, RET_W)),
            pl.BlockSpec((TOKEN_BLOCK, 128), lambda b, t: (t, 0)),
            pl.BlockSpec((TOKEN_BLOCK, 128), lambda b, t: (t, 0)),
            const((4 * RET_L, 256)), const((4 * RET_L, 256)), const((4 * RET_L, 256)), const((4 * RET_L, RET_L)),
            const((1, GROUP)),
        ],
        out_specs=pl.BlockSpec((1, TOKEN_BLOCK, GROUP), lambda b, t: (b, t, 0)),
        out_shape=jax.ShapeDtypeStruct((B, T, GROUP), BF16),
        scratch_shapes=[pltpu.VMEM((256, GROUP), F32)],
        compiler_params=pltpu.CompilerParams(dimension_semantics=("arbitrary", "arbitrary"),
                                             vmem_limit_bytes=VMEM_LIMIT),
        name="retention",
    )(xb, w, cos, sin, jnp.asarray(ms), jnp.asarray(mq), jnp.asarray(dk), jnp.asarray(di), g)


RWKV_L = 64


RWKV_HG = 2
RWKV_GW = RWKV_HG * 64


def _rwkv_tables(n_batch):
    lane = np.arange(RWKV_GW)
    row = np.arange(RWKV_GW)
    ms4 = (lane[None, :] // 64 == row[:, None] // 64).astype(np.float32)
    same = (row[:, None] // 64 == row[None, :] // 64)
    tt, ii = row[:, None] % 64, row[None, :] % 64
    strict = (same & (tt > ii)).astype(np.float32)
    incl = (same & (tt >= ii)).astype(np.float32)
    rt = np.arange(n_batch * TOKEN_BLOCK)
    tri = ((rt[:, None] // 64 == rt[None, :] // 64) & (rt[:, None] >= rt[None, :])).astype(np.float32)
    r5 = np.arange(512)
    bd = (r5[:, None] // 64 == r5[None, :] // 64).astype(np.float32)
    return ms4, strict, incl, tri, bd


def _rwkv_kernel(x_ref, w_ref, mu_ref, w0_ref, w2_ref, a0_ref, a2_ref, g2_ref, kk_ref, ka_ref, rk_ref, gn_ref,
                 ms4_ref, strict_ref, incl_ref, tri_ref, bd_ref, y_ref, p_scr, s_scr):
    NB, TC, L = x_ref.shape[0], TOKEN_BLOCK, RWKV_L
    GW, NG = RWKV_GW, GROUP // RWKV_GW

    @pl.when(pl.program_id(0) == 0)
    def _():
        s_scr[...] = jnp.zeros_like(s_scr)
        for b in range(NB):
            p_scr[b, pl.ds(0, 8), :] = jnp.zeros((8, RWKV_W), F32)

    p = jnp.dot(x_ref[...].reshape(NB * TC, D_MODEL), w_ref[...], preferred_element_type=F32)
    prevs = []
    for b in range(NB):
        pb = p[b * TC:(b + 1) * TC]
        p_scr[b, pl.ds(8, TC), :] = pb
        prevs.append(p_scr[b, pl.ds(7, TC), :])
        p_scr[b, pl.ds(0, 8), :] = pb[TC - 8:, :]
    prev = jnp.concatenate(prevs, axis=0)
    cols = p + (prev - p) * mu_ref[...]
    r, k, v = cols[:, 0:512], cols[:, 512:1024], cols[:, 1024:1536]
    tail = cols[:, 1536:1792]
    w = -jax.nn.softplus(-(w0_ref[...] + _dot(jnp.tanh(tail), w2_ref[...]))) - 0.5
    logdec = -jnp.exp(w)
    a = jax.nn.sigmoid(a0_ref[...] + _dot(tail, a2_ref[...]))
    g = _dot(jax.nn.sigmoid(tail), g2_ref[...])
    kk = k * kk_ref[...]
    bd = bd_ref[...]
    kk = kk / jnp.maximum(jnp.sqrt(_dot_sel(kk * kk, bd, 1)), 1e-12)
    k = k * (1.0 + (a - 1.0) * ka_ref[...])
    bonus = _dot_sel(r * k * rk_ref[...], bd, 1) * v
    cum = _sel_dot(tri_ref[...], logdec, 2)
    beta = kk * a
    ms4 = ms4_ref[...]
    strict, incl = strict_ref[...], incl_ref[...]

    def stack(t):
        return jnp.concatenate([t] * RWKV_HG, axis=0) * ms4

    chains = [(b, gi) for b in range(NB) for gi in range(NG)]
    CH = range(len(chains))
    st = [s_scr[b, gi] for b, gi in chains]
    outs = [[[None] * NG for _ in range(TC // L)] for _ in range(NB)]
    for c in range(TC // L):
        x1, x2, yy, z1, z2, sv, cl = ([None] * len(chains) for _ in range(7))
        for i, (b, gi) in enumerate(chains):
            rows = slice(b * TC + c * L, b * TC + (c + 1) * L)
            lanes = slice(gi * GW, (gi + 1) * GW)
            cm, lw = cum[rows, lanes], logdec[rows, lanes]
            cl[i] = cm[L - 1:L, :]
            gam, gam_ex, igam, gam_r = jnp.exp(cm), jnp.exp(cm - lw), jnp.exp(-cm), jnp.exp(cl[i] - cm)
            kk_c, r_c, k_c, v_c, b_c = kk[rows, lanes], r[rows, lanes], k[rows, lanes], v[rows, lanes], beta[rows, lanes]
            x1[i], x2[i] = stack(-kk_c * gam_ex), stack(r_c * gam)
            yy[i] = jnp.concatenate([stack(b_c * igam), stack(k_c * igam)], axis=0)
            z1[i], z2[i] = stack(b_c * gam_r), stack(k_c * gam_r)
            sv[i] = stack(v_c)
        ab = [_dot_nt(jnp.concatenate([x1[i], x2[i]], axis=0), yy[i]) for i in CH]
        a_b = [ab[i][0:GW, 0:GW] * strict for i in CH]
        a_k = [ab[i][0:GW, GW:2 * GW] * strict for i in CH]
        b_b = [ab[i][GW:2 * GW, 0:GW] * incl for i in CH]
        b_k = [ab[i][GW:2 * GW, GW:2 * GW] * incl for i in CH]
        u = [_dot(a_k[i], sv[i]) for i in CH]
        t0 = [_dot_nt(x1[i], st[i]) for i in CH]
        u = [u[i] + t0[i] for i in CH]
        pw = a_b
        for it in range(6):
            t = [_dot(pw[i], u[i]) for i in CH]
            if it < 5:
                pw = [_dot(pw[i], pw[i]) for i in CH]
            u = [u[i] + t[i] for i in CH]
        y0 = [_dot_nt(x2[i], st[i]) for i in CH]
        y1 = [_dot(b_b[i], u[i]) for i in CH]
        y2 = [_dot(b_k[i], sv[i]) for i in CH]
        s1 = [_dot_tn(u[i], z1[i]) for i in CH]
        s2 = [_dot_tn(sv[i], z2[i]) for i in CH]
        for i, (b, gi) in enumerate(chains):
            yst = y0[i] + y1[i] + y2[i]
            outs[b][c][gi] = sum(yst[h * L:(h + 1) * L] for h in range(RWKV_HG))
            st[i] = st[i] * jnp.exp(cl[i]) + s1[i] + s2[i]
    for i, (b, gi) in enumerate(chains):
        s_scr[b, gi] = st[i]

    y = jnp.concatenate([jnp.concatenate(o, axis=1) for ob in outs for o in ob], axis=0)
    mean = _dot_sel(y, bd, 1) * (1.0 / 64.0)
    d = y - mean
    var = _dot_sel(d * d, bd, 1) * (1.0 / 64.0)
    yn = d * lax.rsqrt(var + NORM_EPS) * gn_ref[...]
    y_ref[...] = ((yn + bonus) * g).astype(y_ref.dtype).reshape(NB, TC, GROUP)


def _rwkv(xb, w, mu, w0, w2p, a0, a2p, g2p, k_k, k_a, r_k, gn, tables):
    B, T, _ = xb.shape
    const = lambda shape: pl.BlockSpec(shape, lambda t: (0,) * len(shape))
    row = lambda n: const((1, n))
    return pl.pallas_call(
        _rwkv_kernel,
        grid=(T // TOKEN_BLOCK,),
        in_specs=[
            pl.BlockSpec((B, TOKEN_BLOCK, D_MODEL), lambda t: (0, t, 0)),
            const((D_MODEL, RWKV_W)), row(RWKV_W), row(GROUP), const((256, GROUP)), row(GROUP), const((256, GROUP)),
            const((256, GROUP)), row(GROUP), row(GROUP), row(GROUP), row(GROUP),
            const((RWKV_GW, RWKV_GW)), const((RWKV_GW, RWKV_GW)), const((RWKV_GW, RWKV_GW)),
            const((B * TOKEN_BLOCK, B * TOKEN_BLOCK)), const((GROUP, GROUP)),
        ],
        out_specs=pl.BlockSpec((B, TOKEN_BLOCK, GROUP), lambda t: (0, t, 0)),
        out_shape=jax.ShapeDtypeStruct((B, T, GROUP), BF16),
        scratch_shapes=[pltpu.VMEM((B, TOKEN_BLOCK + 8, RWKV_W), F32),
                        pltpu.VMEM((B, GROUP // RWKV_GW, RWKV_GW, RWKV_GW), F32)],
        compiler_params=pltpu.CompilerParams(dimension_semantics=("arbitrary",), vmem_limit_bytes=VMEM_LIMIT),
        name="rwkv7",
    )(xb, w, mu, w0, w2p, a0, a2p, g2p, k_k, k_a, r_k, gn,
      *[jnp.asarray(t, dt) for t, dt in zip(tables(B), (F32, F32, F32, BF16, BF16))])


GLA_SUB = 16


def _gla_tables():
    r = np.arange(256)
    tri16 = ((r[:, None] // 16 == r[None, :] // 16) & (r[:, None] >= r[None, :])).astype(np.float32)
    sel = (r[:, None] // 64 == np.arange(512)[None, :] // 128).astype(np.float32)
    ms = (np.arange(256)[None, :] // 64 == np.arange(64)[:, None] // 16).astype(np.float32)
    return tri16, sel, ms


def _gla_kernel(x_ref, w_ref, g2_ref, gb_ref, gn_ref, tri_ref, sel_ref, ms_ref, y_ref, st_scr):
    TC, SUB = TOKEN_BLOCK, GLA_SUB

    @pl.when(pl.program_id(1) == 0)
    def _():
        st_scr[...] = jnp.zeros_like(st_scr)

    p = jnp.dot(x_ref[0], w_ref[...], preferred_element_type=F32)
    gate = p[:, 1024:1536]
    la = jax.nn.log_sigmoid(_dot(p[:, 1536:1664], g2_ref[...]) + gb_ref[...]) * (1.0 / GLA_TAU)
    q, k, v = p[:, 0:256] * (64.0 ** -0.5), p[:, 256:512], p[:, 512:1024]
    b = _sel_dot(tri_ref[...], la, 3)
    sel = sel_ref[...]
    ms = ms_ref[...]
    row = lax.broadcasted_iota(jnp.int32, (SUB, 1), 0)
    outs = []
    st = st_scr[...]
    for s in range(TC // SUB):
        rows = slice(s * SUB, (s + 1) * SUB)
        qb, kb, vb, bb = q[rows], k[rows], v[rows], b[rows]
        bl = bb[SUB - 1:SUB, :]
        qe = jnp.concatenate([qb * jnp.exp(bb)] * 4, axis=0) * ms
        o_inter = jnp.concatenate(
            [_dot_nt(qe[h * SUB:(h + 1) * SUB], st[h * 128:(h + 1) * 128, :]) for h in range(4)], axis=1)
        ts = []
        for j in range(SUB):
            diff = jnp.where(row >= j, bb - bb[j:j + 1, :], -1e30)
            ts.append(qb * kb[j:j + 1, :] * jnp.exp(diff))
        rr = jnp.dot(jnp.concatenate(ts, axis=0).astype(BF16), sel, preferred_element_type=F32)
        o_intra = rr[0:SUB] * vb[0:1, :]
        for j in range(1, SUB):
            o_intra = o_intra + rr[j * SUB:(j + 1) * SUB] * vb[j:j + 1, :]
        outs.append(o_intra + o_inter)
        st = st * jnp.exp(bl) + _dot_tn(vb, kb * jnp.exp(bl - bb))
    st_scr[...] = st
    o = _head_norm_lanes(jnp.concatenate(outs, axis=0), 128)
    y_ref[0] = (o * gn_ref[...] * _silu(gate)).astype(y_ref.dtype)


def _gla(xb, w, g2p, gb, gn, tables):
    B, T, _ = xb.shape
    const = lambda shape: pl.BlockSpec(shape, lambda b, t: (0,) * len(shape))
    return pl.pallas_call(
        _gla_kernel,
        grid=(B, T // TOKEN_BLOCK),
        in_specs=[
            pl.BlockSpec((1, TOKEN_BLOCK, D_MODEL), lambda b, t: (b, t, 0)),
            const((D_MODEL, GLA_W)), const((128, 256)), const((1, 256)), const((1, GROUP)),
            const((256, 256)), const((256, 512)), const((64, 256)),
        ],
        out_specs=pl.BlockSpec((1, TOKEN_BLOCK, GROUP), lambda b, t: (b, t, 0)),
        out_shape=jax.ShapeDtypeStruct((B, T, GROUP), BF16),
        scratch_shapes=[pltpu.VMEM((GROUP, 256), F32)],
        compiler_params=pltpu.CompilerParams(dimension_semantics=("arbitrary", "arbitrary"),
                                             vmem_limit_bytes=VMEM_LIMIT),
        name="gla",
    )(xb, w, g2p, gb, gn, *[jnp.asarray(t, dt) for t, dt in zip(tables, (BF16, BF16, F32))])


MLSTM_L = 128


def _mlstm_tables():
    r = np.arange(TOKEN_BLOCK)
    tri = ((r[:, None] // MLSTM_L == r[None, :] // MLSTM_L) & (r[:, None] >= r[None, :])).astype(np.float32)
    ms = (np.arange(256)[None, :] // 64 == np.arange(4 * MLSTM_L)[:, None] // MLSTM_L).astype(np.float32)
    return tri, ms


def _mlstm_kernel(x_ref, w_ref, conv_ref, gbias_ref, gn_ref, tri_ref, ms_ref, y_ref,
                  qk_scr, c_scr, n_scr, m_scr):
    TC, L = TOKEN_BLOCK, MLSTM_L

    @pl.when(pl.program_id(1) == 0)
    def _():
        c_scr[...] = jnp.zeros_like(c_scr)
        n_scr[...] = jnp.zeros_like(n_scr)
        m_scr[...] = jnp.zeros_like(m_scr)
        qk_scr[pl.ds(0, 8), :] = jnp.zeros((8, 512), F32)

    p = jnp.dot(x_ref[0], w_ref[...], preferred_element_type=F32)
    qk_scr[pl.ds(8, TC), :] = p[:, 0:512]
    conv = p[:, 0:512] * conv_ref[3:4, :]
    for j in range(3):
        conv = conv + qk_scr[pl.ds(5 + j, TC), :] * conv_ref[j:j + 1, :]
    qk_scr[pl.ds(0, 8), :] = p[TC - 8:, 0:512]
    qk = _silu(conv)
    q, k = qk[:, 0:256], qk[:, 256:512] * (64.0 ** -0.5)
    v, og = p[:, 512:1024], p[:, 1024:1536]
    tail = p[:, 1536:1664] + gbias_ref[...]
    lane = lax.broadcasted_iota(jnp.int32, (1, 128), 1)
    logf = jnp.where((lane >= 4) & (lane < 8), jax.nn.log_sigmoid(tail), 0.0)
    cumf = _sel_dot(tri_ref[...], logf, 3)
    gcol = jnp.where(lane < 4, tail, cumf)
    eye = (lax.broadcasted_iota(jnp.int32, (128, 128), 0) == lax.broadcasted_iota(jnp.int32, (128, 128), 1)).astype(BF16)
    ms = ms_ref[...]
    ri = lax.broadcasted_iota(jnp.int32, (L, L), 0)
    ci = lax.broadcasted_iota(jnp.int32, (L, L), 1)
    causal = ri >= ci
    lane256 = lax.broadcasted_iota(jnp.int32, (1, 256), 1)
    c_state = [c_scr[:, h * 128:(h + 1) * 128] for h in range(4)]
    m_state = [m_scr[0:1, h:h + 1] for h in range(4)]
    n_state = n_scr[...]

    for c in range(TC // L):
        rows = slice(c * L, (c + 1) * L)
        gc = gcol[rows, :]
        gr = sum(lax.dot_general(eye, piece, (((1,), (1,)), ((), ())), preferred_element_type=F32)
                 for piece in _split(gc, 3))
        qc, kc, vc = q[rows], k[rows], v[rows]
        q4 = jnp.concatenate([qc] * 4, axis=0) * ms
        scores = _dot_nt(q4, kc)
        qn = jnp.sum(q4 * n_state, axis=1, keepdims=True)
        hbs = [slice(h * L, (h + 1) * L) for h in range(4)]
        vhs = [vc[:, h * 128:(h + 1) * 128] for h in range(4)]
        qc_state = [_dot(q4[hbs[h]], c_state[h]) for h in range(4)]
        ss, inters, m_rows, wj_cols, wrows, scales = [], [], [], [], [], []
        n_scale = jnp.zeros((1, 256), F32)
        for h in range(4):
            b_col, i_col = gc[:, 4 + h:5 + h], gc[:, h:h + 1]
            b_row, i_row = gr[4 + h:5 + h, :], gr[h:h + 1, :]
            m_prev = m_state[h]
            log_d = jnp.where(causal, b_col - b_row + i_row, -jnp.inf)
            m_inter = b_col + m_prev
            m_row = jnp.maximum(jnp.max(log_d, axis=-1, keepdims=True), m_inter)
            ss.append(scores[hbs[h]] * jnp.exp(log_d - m_row))
            inters.append(jnp.exp(m_inter - m_row))
            m_rows.append(m_row)
            m_new = m_row[L - 1:L, :]
            b_last = b_col[L - 1:L, :]
            wj_cols.append(jnp.exp(b_last - b_col + i_col - m_new))
            wrows.append(jnp.exp(b_last - b_row + i_row - m_new))
            scale = jnp.exp(b_last + m_prev - m_new)
            scales.append(scale)
            n_scale = jnp.where(lane256 // 64 == h, scale, n_scale)
            m_state[h] = m_new
        sv = [_dot(ss[h], vhs[h]) for h in range(4)]
        upd = [_dot_tn(kc * wj_cols[h], vhs[h]) for h in range(4)]
        outs = []
        for h in range(4):
            num = sv[h] + inters[h] * qc_state[h]
            den = jnp.sum(ss[h], axis=-1, keepdims=True) + inters[h] * qn[hbs[h]]
            outs.append(num / jnp.maximum(jnp.abs(den), jnp.exp(-m_rows[h])))
            c_state[h] = scales[h] * c_state[h] + upd[h]
        wk = _dot(jnp.concatenate(wrows + [jnp.zeros((4, L), F32)], axis=0), kc)
        n_add = jnp.zeros((1, 256), F32)
        for h in range(4):
            n_add = jnp.where(lane256 // 64 == h, wk[h:h + 1, :], n_add)
        n_state = n_scale * n_state + n_add
        o = _head_norm_lanes(jnp.concatenate(outs, axis=1), 128)
        y_ref[0, rows, :] = (o * gn_ref[...] * jax.nn.sigmoid(og[rows])).astype(y_ref.dtype)

    n_scr[...] = n_state
    for h in range(4):
        c_scr[:, h * 128:(h + 1) * 128] = c_state[h]
        m_scr[0:1, h:h + 1] = m_state[h]


def _mlstm(xb, w, conv_w, gbias, gn, tables):
    B, T, _ = xb.shape
    const = lambda shape: pl.BlockSpec(shape, lambda b, t: (0,) * len(shape))
    return pl.pallas_call(
        _mlstm_kernel,
        grid=(B, T // TOKEN_BLOCK),
        in_specs=[
            pl.BlockSpec((1, TOKEN_BLOCK, D_MODEL), lambda b, t: (b, t, 0)),
            const((D_MODEL, MLSTM_W)), const((4, 512)), const((1, 128)), const((1, GROUP)),
            const((TOKEN_BLOCK, TOKEN_BLOCK)), const((4 * MLSTM_L, 256)),
        ],
        out_specs=pl.BlockSpec((1, TOKEN_BLOCK, GROUP), lambda b, t: (b, t, 0)),
        out_shape=jax.ShapeDtypeStruct((B, T, GROUP), BF16),
        scratch_shapes=[pltpu.VMEM((TOKEN_BLOCK + 8, 512), F32), pltpu.VMEM((256, GROUP), F32),
                        pltpu.VMEM((1, 256), F32), pltpu.VMEM((8, 128), F32)],
        compiler_params=pltpu.CompilerParams(dimension_semantics=("arbitrary", "arbitrary"),
                                             vmem_limit_bytes=VMEM_LIMIT),
        name="mlstm",
    )(xb, w, conv_w, gbias, gn, *[jnp.asarray(t, dt) for t, dt in zip(tables, (BF16, F32))])


ROW_BLOCK = 512


def _layer_norm(z, g, b):
    mu = jnp.mean(z, axis=-1, keepdims=True)
    d = z - mu
    var = jnp.mean(d * d, axis=-1, keepdims=True)
    return d * lax.rsqrt(var + NORM_EPS) * g + b


def _outproj_kernel(y0_ref, y1_ref, y2_ref, y3_ref, x_ref, w_ref, g_ref, b_ref, rwh_ref, rwl_ref, rb_ref,
                    x1_ref, x1b_ref, idx_ref, gate_ref):
    acc = jnp.dot(y0_ref[...], w_ref[0:512, :], preferred_element_type=F32)
    acc += jnp.dot(y1_ref[...], w_ref[512:1024, :], preferred_element_type=F32)
    acc += jnp.dot(y2_ref[...], w_ref[1024:1536, :], preferred_element_type=F32)
    acc += jnp.dot(y3_ref[...], w_ref[1536:2048, :], preferred_element_type=F32)
    x1 = _layer_norm(DN_ALPHA * x_ref[...] + acc, g_ref[...], b_ref[...])
    x1_ref[...] = x1
    x1b_ref[...] = x1.astype(BF16)
    x_hi, x_lo = _split(x1, 2)
    logits = (jnp.dot(x_hi, rwh_ref[...], preferred_element_type=F32)
              + jnp.dot(x_lo, rwh_ref[...], preferred_element_type=F32)
              + jnp.dot(x_hi, rwl_ref[...], preferred_element_type=F32)) + rb_ref[...]
    lane = lax.broadcasted_iota(jnp.int32, logits.shape, 1)
    logits = jnp.where(lane < N_EXPERTS, logits, -jnp.inf)
    idx_out = jnp.zeros(logits.shape, jnp.int32)
    gate_out = jnp.zeros(logits.shape, F32)
    top = None
    for kth in range(TOP_K):
        m = jnp.max(logits, axis=-1, keepdims=True)
        sel = jnp.min(jnp.where(logits == m, lane, LANE), axis=-1, keepdims=True)
        if kth == 0:
            top = m
        idx_out = jnp.where(lane == kth, sel, idx_out)
        gate_out = jnp.where(lane == kth, jnp.exp(m - top), gate_out)
        logits = jnp.where(lane == sel, -jnp.inf, logits)
    gate_out = gate_out / jnp.sum(gate_out, axis=-1, keepdims=True)
    idx_ref[...] = idx_out
    gate_ref[...] = gate_out


def _outproj(ys, x, w_out, g, b, rw, rb):
    N = x.shape[0]
    rb_ = ROW_BLOCK
    const = lambda shape: pl.BlockSpec(shape, lambda i: (0,) * len(shape))
    rows = lambda width: pl.BlockSpec((rb_, width), lambda i: (i, 0))
    rw_hi = rw.astype(BF16)
    rw_lo = (rw - rw_hi.astype(F32)).astype(BF16)
    return pl.pallas_call(
        _outproj_kernel,
        grid=(N // rb_,),
        in_specs=[rows(GROUP)] * 4 + [rows(D_MODEL), const((D_MODEL, D_MODEL)), const((1, D_MODEL)), const((1, D_MODEL)),
                                      const((D_MODEL, LANE)), const((D_MODEL, LANE)), const((1, LANE))],
        out_specs=[rows(D_MODEL), rows(D_MODEL), rows(LANE), rows(LANE)],
        out_shape=[jax.ShapeDtypeStruct((N, D_MODEL), F32), jax.ShapeDtypeStruct((N, D_MODEL), BF16),
                   jax.ShapeDtypeStruct((N, LANE), jnp.int32), jax.ShapeDtypeStruct((N, LANE), F32)],
        compiler_params=pltpu.CompilerParams(dimension_semantics=("arbitrary",), vmem_limit_bytes=VMEM_LIMIT),
        name="outproj_norm_router",
    )(*ys, x, w_out, g, b, rw_hi, rw_lo, rb)


MOE_BLOCK = 512
FF_TILE = 1024
DN_TILE = 1024
MOE_SPLIT = 6


def _count_le(bounds, q):
    return jnp.sum((q[:, None] >= bounds[None, :]).astype(jnp.int32), axis=1)


def _ffn_schedule(pcounts, pstarts, lo, hi, n_tiles):
    b0 = jnp.clip(pstarts // MOE_BLOCK, lo, hi).astype(jnp.int32)
    b1 = jnp.clip((pstarts + pcounts) // MOE_BLOCK, lo, hi).astype(jnp.int32)
    nblk, bstart = b1 - b0, b0 - lo
    steps_e = nblk * n_tiles
    ends = jnp.cumsum(steps_e).astype(jnp.int32)
    begins = ends - steps_e
    total = ends[-1]
    s = jnp.arange((hi - lo) * n_tiles, dtype=jnp.int32)
    e = jnp.minimum(_count_le(ends, s), N_EXPERTS - 1)
    r = s - begins[e]
    nb_e = jnp.maximum(nblk[e], 1)
    til, j = r // nb_e, r % nb_e
    blk = bstart[e] + j
    last = jnp.maximum(total - 1, 0)
    valid = s < total
    rp = s - total
    blk = jnp.where(valid, blk, total // n_tiles + rp // n_tiles)
    otil = jnp.where(valid, til, rp % n_tiles)
    wtil = jnp.where(valid, til, til[last])
    e = jnp.where(valid, e, e[last])
    first = (valid & (j == 0)).astype(jnp.int32)
    i32 = lambda a: a.astype(jnp.int32)
    return i32(blk), i32(otil), i32(wtil), i32(e), first, i32(total).reshape(1)


def _ffn_up_kernel(blk_ref, otil_ref, wtil_ref, ex_ref, first_ref, ns_ref,
                   xs_ref, wg_ref, wl_ref, bg_ref, bl_ref, act_ref, wg_bf, wl_bf):
    s = pl.program_id(0)

    @pl.when(first_ref[s] == 1)
    def _():
        wg_bf[...] = wg_ref[...].astype(BF16)
        wl_bf[...] = wl_ref[...].astype(BF16)

    @pl.when(s < ns_ref[0])
    def _():
        xs = xs_ref[...]
        glu = jnp.dot(xs, wg_bf[...], preferred_element_type=F32) + bg_ref[...]
        lin = jnp.dot(xs, wl_bf[...], preferred_element_type=F32) + bl_ref[...]
        glu = jnp.minimum(glu, SWIGLU_LIMIT)
        lin = jnp.clip(lin, -SWIGLU_LIMIT, SWIGLU_LIMIT)
        act_ref[...] = (glu * jax.nn.sigmoid(SWIGLU_ALPHA * glu) * (lin + 1.0)).astype(act_ref.dtype)

    @pl.when(s >= ns_ref[0])
    def _():
        act_ref[...] = jnp.zeros_like(act_ref)


def _ffn_dn_kernel(blk_ref, otil_ref, wtil_ref, ex_ref, first_ref, ns_ref, *refs, ranges):
    act_refs = refs[:len(ranges)]
    wd_ref, bd_ref, o_ref, wd_bf = refs[len(ranges):]
    s = pl.program_id(0)
    blk = blk_ref[s]

    @pl.when(first_ref[s] == 1)
    def _():
        wd_bf[...] = wd_ref[...].astype(BF16)

    for act_ref, (lo, hi) in zip(act_refs, ranges):
        @pl.when((s < ns_ref[0]) & (blk >= lo) & (blk < hi))
        def _():
            out = jnp.dot(act_ref[...], wd_bf[...], preferred_element_type=F32) + bd_ref[...]
            o_ref[...] = out.astype(o_ref.dtype)

    @pl.when(s >= ns_ref[0])
    def _():
        o_ref[...] = jnp.zeros_like(o_ref)


def _sched_maps(layer):
    wtile = lambda off: (lambda s, blk, otil, wtil, ex, first, ns: (layer, ex[s], 0, off + wtil[s]))
    rows_full = lambda s, blk, otil, wtil, ex, first, ns: (blk[s], 0)
    rows_tile = lambda s, blk, otil, wtil, ex, first, ns: (blk[s], otil[s])
    return wtile, rows_full, rows_tile


def _expert_up(layer, xs, sched, w_gu, b_gu):
    n_rows = xs.shape[0]
    nt = D_FF // FF_TILE
    wtile, rows_full, rows_tile = _sched_maps(layer)
    return pl.pallas_call(
        _ffn_up_kernel,
        grid_spec=pltpu.PrefetchScalarGridSpec(
            num_scalar_prefetch=6,
            grid=((n_rows // MOE_BLOCK) * nt,),
            in_specs=[
                pl.BlockSpec((MOE_BLOCK, D_MODEL), rows_full),
                pl.BlockSpec((None, None, D_MODEL, FF_TILE), wtile(0)),
                pl.BlockSpec((None, None, D_MODEL, FF_TILE), wtile(nt)),
                pl.BlockSpec((None, None, 1, FF_TILE), wtile(0)),
                pl.BlockSpec((None, None, 1, FF_TILE), wtile(nt)),
            ],
            out_specs=pl.BlockSpec((MOE_BLOCK, FF_TILE), rows_tile),
            scratch_shapes=[pltpu.VMEM((D_MODEL, FF_TILE), BF16), pltpu.VMEM((D_MODEL, FF_TILE), BF16)],
        ),
        out_shape=jax.ShapeDtypeStruct((n_rows, D_FF), BF16),
        compiler_params=pltpu.CompilerParams(dimension_semantics=("arbitrary",), vmem_limit_bytes=VMEM_LIMIT),
        name="expert_up",
    )(*sched, xs, w_gu, w_gu, b_gu, b_gu)


def _expert_down(layer, acts, ranges, sched, w_dn, b_dn):
    n_blocks = ranges[-1][1]
    wtile, rows_full, rows_tile = _sched_maps(layer)
    part = lambda lo, hi: (lambda s, blk, otil, wtil, ex, first, ns: (jnp.clip(blk[s] - lo, 0, hi - lo - 1), 0))
    return pl.pallas_call(
        functools.partial(_ffn_dn_kernel, ranges=tuple(ranges)),
        grid_spec=pltpu.PrefetchScalarGridSpec(
            num_scalar_prefetch=6,
            grid=(n_blocks * (D_MODEL // DN_TILE),),
            in_specs=[pl.BlockSpec((MOE_BLOCK, D_FF), part(lo, hi)) for lo, hi in ranges] + [
                pl.BlockSpec((None, None, D_FF, DN_TILE), wtile(0)),
                pl.BlockSpec((None, None, 1, DN_TILE), wtile(0)),
            ],
            out_specs=pl.BlockSpec((MOE_BLOCK, DN_TILE), rows_tile),
            scratch_shapes=[pltpu.VMEM((D_FF, DN_TILE), BF16)],
        ),
        out_shape=jax.ShapeDtypeStruct((n_blocks * MOE_BLOCK, D_MODEL), BF16),
        compiler_params=pltpu.CompilerParams(dimension_semantics=("arbitrary",), vmem_limit_bytes=VMEM_LIMIT),
        name="expert_down",
    )(*sched, *acts, w_dn, b_dn)


def _combine_kernel(x_ref, y0_ref, y1_ref, y2_ref, y3_ref, gate_ref, g_ref, b_ref, o_ref, ob_ref):
    gate = gate_ref[...]
    y = ((y0_ref[...].astype(F32) * gate[:, 0:1] + y1_ref[...].astype(F32) * gate[:, 1:2])
         + (y2_ref[...].astype(F32) * gate[:, 2:3] + y3_ref[...].astype(F32) * gate[:, 3:4]))
    x2 = _layer_norm(DN_ALPHA * x_ref[...] + y, g_ref[...], b_ref[...])
    o_ref[...] = x2
    ob_ref[...] = x2.astype(BF16)


def _combine(x1, ysg, gate, g, b):
    N = x1.shape[0]
    rb_ = ROW_BLOCK
    const = lambda shape: pl.BlockSpec(shape, lambda i: (0,) * len(shape))
    rows = pl.BlockSpec((rb_, D_MODEL), lambda i: (i, 0))
    return pl.pallas_call(
        _combine_kernel,
        grid=(N // rb_,),
        in_specs=[rows] * (1 + TOP_K) + [pl.BlockSpec((rb_, LANE), lambda i: (i, 0)),
                                         const((1, D_MODEL)), const((1, D_MODEL))],
        out_specs=[rows] * 2,
        out_shape=[jax.ShapeDtypeStruct((N, D_MODEL), F32), jax.ShapeDtypeStruct((N, D_MODEL), BF16)],
        compiler_params=pltpu.CompilerParams(dimension_semantics=("arbitrary",), vmem_limit_bytes=VMEM_LIMIT),
        name="combine_norm",
    )(x1, *ysg, gate, g, b)


def _route(top_idx):
    n_tok = top_idx.shape[0]
    n_assign = n_tok * TOP_K
    flat_e = top_idx.reshape(-1).astype(jnp.int32)
    iota = jnp.arange(n_assign, dtype=jnp.int32)
    experts = jnp.arange(N_EXPERTS, dtype=jnp.int32)
    counts = jnp.sum((flat_e[:, None] == experts[None, :]).astype(jnp.int32), axis=0)
    starts = jnp.cumsum(counts) - counts
    pcounts = (counts + MOE_BLOCK - 1) // MOE_BLOCK * MOE_BLOCK
    pends = jnp.cumsum(pcounts)
    pstarts = pends - pcounts
    _, order = lax.sort((flat_e, iota), num_keys=1)
    shift = jnp.sum(jnp.where(iota[:, None] >= (starts + counts)[None, :], (pcounts - counts)[None, :], 0), axis=1)
    dest = (iota + shift).astype(jnp.int32)
    n_slots = -(-n_assign // MOE_BLOCK) * MOE_BLOCK + N_EXPERTS * MOE_BLOCK
    n_blocks = n_slots // MOE_BLOCK
    block_start = jnp.arange(n_blocks, dtype=jnp.int32) * MOE_BLOCK
    block_exp = jnp.minimum(_count_le(pends, block_start), N_EXPERTS - 1)
    boff = block_start - pstarts[block_exp]
    first_a = starts[block_exp] + boff
    n_valid = jnp.clip(counts[block_exp] - boff, 0, MOE_BLOCK)
    j = jnp.arange(MOE_BLOCK, dtype=jnp.int32)
    valid = j[None, :] < n_valid[:, None]
    src = order[jnp.clip(first_a[:, None] + j[None, :], 0, n_assign - 1)]
    filler = (block_start[:, None] + j[None, :]) % n_tok
    slot_tok = jnp.where(valid, src // TOP_K, filler).astype(jnp.int32).reshape(-1)
    _, pos = lax.sort((order, dest), num_keys=1)
    ranges = [(p * n_blocks // MOE_SPLIT, (p + 1) * n_blocks // MOE_SPLIT) for p in range(MOE_SPLIT)]
    sched_up = [_ffn_schedule(pcounts, pstarts, lo, hi, D_FF // FF_TILE) for lo, hi in ranges]
    sched_dn = _ffn_schedule(pcounts, pstarts, 0, n_blocks, D_MODEL // DN_TILE)
    return slot_tok, pos, ranges, sched_up, sched_dn


def _pad_rows(w, row0, total):
    return jnp.zeros((total, w.shape[1]), F32).at[row0:row0 + w.shape[0]].set(w)


def kernel(x, w_in, rwkv_mu, rwkv_w0, rwkv_w2, rwkv_a0, rwkv_a2, rwkv_g2, rwkv_k_k, rwkv_k_a, rwkv_r_k, gla_g2, gla_gb, mlstm_conv, mlstm_ib, mlstm_fb, head_norm_g, w_out, ln1_g, ln1_b, router_w, router_b, w_gu, b_gu, w_dn, b_dn, ln2_g, ln2_b):
    B, T, D = x.shape
    N = B * T
    depth = w_in.shape[0]
    ret_tab, rwkv_tab, gla_tab, mlstm_tab = _ret_tables(), _rwkv_tables, _gla_tables(), _mlstm_tables()
    pos_t = jnp.arange(T, dtype=F32)
    inv = ROPE_BASE ** (-jnp.arange(32, dtype=F32) / 32)
    ang = pos_t[:, None] * inv[None, :]
    cos = jnp.tile(jnp.cos(ang), (1, 4))
    sin = jnp.tile(jnp.sin(ang), (1, 4))
    row = lambda v: v.reshape(1, -1)

    b_gu4 = b_gu.reshape(depth, N_EXPERTS, 1, 2 * D_FF)
    b_dn4 = b_dn.reshape(depth, N_EXPERTS, 1, D_MODEL)
    take_rows = lambda a, i: a.at[i].get(mode='promise_in_bounds')

    xf = x.reshape(N, D)
    xb = x.astype(BF16)
    for l in range(depth):
        w_ext = jnp.concatenate([w_in[l], jnp.zeros((D, 1), F32)], axis=1)
        seg = lambda cols: jnp.take(w_ext, jnp.asarray(cols), axis=1).astype(BF16)
        gn = head_norm_g[l]
        y_ret = _retention(xb, seg(_ret_cols()), cos, sin, row(gn[0:512]), ret_tab)
        mu = jnp.concatenate([rwkv_mu[l], jnp.zeros((96,), F32)])
        y_rwkv = _rwkv(xb, seg(_rwkv_cols()), row(mu), row(rwkv_w0[l]), _pad_rows(rwkv_w2[l], 0, 256),
                       row(rwkv_a0[l]), _pad_rows(rwkv_a2[l], 32, 256), _pad_rows(rwkv_g2[l], 64, 256),
                       row(rwkv_k_k[l]), row(rwkv_k_a[l]), row(rwkv_r_k[l]), row(gn[512:1024]), rwkv_tab)
        y_gla = _gla(xb, seg(_gla_cols()), _pad_rows(gla_g2[l], 0, 128), row(gla_gb[l]), row(gn[1024:1536]), gla_tab)
        gbias = jnp.concatenate([mlstm_ib[l], mlstm_fb[l], jnp.zeros((120,), F32)])
        y_mlstm = _mlstm(xb, seg(_mlstm_cols()), mlstm_conv[l], row(gbias), row(gn[1536:2048]), mlstm_tab)
        ys = [y.reshape(N, GROUP) for y in (y_ret, y_rwkv, y_gla, y_mlstm)]
        rw = jnp.zeros((D, LANE), F32).at[:, :N_EXPERTS].set(router_w[l])
        rb = jnp.zeros((1, LANE), F32).at[0, :N_EXPERTS].set(router_b[l])
        x1, x1b, idx, gate = _outproj(ys, xf, w_out[l].astype(BF16), row(ln1_g[l]), row(ln1_b[l]), rw, rb)
        slot_tok, pos, ranges, sched_up, sched_dn = _route(idx[:, :TOP_K])
        acts = [_expert_up(l, take_rows(x1b, slot_tok[lo * MOE_BLOCK:hi * MOE_BLOCK]), sched, w_gu, b_gu4)
                for (lo, hi), sched in zip(ranges, sched_up)]
        ys_moe = _expert_down(l, acts, ranges, sched_dn, w_dn, b_dn4)
        pos = pos.reshape(N, TOP_K)
        ysg = [take_rows(ys_moe, pos[:, kth]) for kth in range(TOP_K)]
        xf, xb2 = _combine(x1, ysg, gate, row(ln2_g[l]), row(ln2_b[l]))
        xb = xb2.reshape(B, T, D)
    return xf.reshape(B, T, D)
```
